```python
import math
import jax, jax.numpy as jnp
from jax import lax
import numpy as np

D_MODEL = 1024
BATCH = 2
SEQ = 8192
DEPTH = 2

CHUNK = 64
N_MEM = 256
N_GROUPS = 4
GROUP = D_MODEL // N_GROUPS
D_MIX = N_GROUPS * GROUP
N_HEADS = 4
HEAD_DIM = GROUP // N_HEADS
RWKV_LORA = 32
CONV_WIDTH = 31
SB_BLOCK = 128
XATTN_HEADS = 4
XATTN_HEAD_DIM = D_MODEL // XATTN_HEADS
RMS_EPS = 1e-6
LN_EPS = 1e-5
RWKV_LN_EPS = 64e-5

N_RWKV_SHIFT = 3 * GROUP + 2 * RWKV_LORA
N_RWKV = N_RWKV_SHIFT + GROUP
N_CONV = 3 * GROUP
N_HGRN = 4 * GROUP
N_SB = 4 * GROUP
N_IN = N_RWKV + N_CONV + N_HGRN + N_SB

kernel_name = 'hybrid_parallel_heads_rwkv7_conformer_hgrn2_stickbreak'


def rms_norm(x, g):
    x32 = x.astype(jnp.float32)
    y = x32 * lax.rsqrt(jnp.mean(x32 * x32, axis=-1, keepdims=True) + RMS_EPS)
    return (y * g.astype(jnp.float32)).astype(x.dtype)


def layer_norm_f32(x, g, b, eps):
    x32 = x.astype(jnp.float32)
    mu = jnp.mean(x32, axis=-1, keepdims=True)
    xc = x32 - mu
    var = jnp.mean(xc * xc, axis=-1, keepdims=True)
    return xc * lax.rsqrt(var + eps) * g + b


def split_heads(t):
    return t.reshape(*t.shape[:-1], N_HEADS, HEAD_DIM)


def rwkv7_mixer(p_shift, gate, mu, w0, w_up, a0, a_up, k_k, k_a, r_k, ln_g, ln_b):
    B, T, _ = p_shift.shape
    p = p_shift.astype(jnp.float32)
    p_prev = jnp.pad(p, ((0, 0), (1, 0), (0, 0)))[:, :-1]
    m = p + (p_prev - p) * mu
    r, k, v, xw, xa = jnp.split(m, [GROUP, 2 * GROUP, 3 * GROUP, 3 * GROUP + RWKV_LORA], axis=-1)
    w_log = -jax.nn.softplus(-(w0 + jnp.tanh(xw) @ w_up)) - 0.5
    decay = jnp.exp(-jnp.exp(w_log))
    a = jax.nn.sigmoid(a0 + xa @ a_up)
    kk = split_heads(k * k_k)
    kk = kk / jnp.maximum(jnp.sqrt(jnp.sum(kk * kk, axis=-1, keepdims=True)), 1e-12)
    k = k * (1.0 + (a - 1.0) * k_a)
    r, k, v, decay, a = (split_heads(t) for t in (r, k, v, decay, a))
    xs = tuple(jnp.moveaxis(t, 1, 0) for t in (r, decay, k, v, -kk, kk * a))

    def step(S, inp):
        r_t, w_t, k_t, v_t, a_t, b_t = inp
        sa = jnp.einsum('bhij,bhj->bhi', S, a_t)
        S = S * w_t[:, :, None, :] + sa[..., None] * b_t[:, :, None, :] + v_t[..., None] * k_t[:, :, None, :]
        y = jnp.einsum('bhij,bhj->bhi', S, r_t)
        return S, y

    S0 = jnp.zeros((B, N_HEADS, HEAD_DIM, HEAD_DIM), jnp.float32)
    _, y = lax.scan(step, S0, xs)
    y = jnp.moveaxis(y, 0, 1)
    y = layer_norm_f32(y, ln_g.reshape(N_HEADS, HEAD_DIM), ln_b.reshape(N_HEADS, HEAD_DIM), RWKV_LN_EPS)
    bonus = jnp.sum(r * k * r_k, axis=-1, keepdims=True) * v
    out = (y + bonus).reshape(B, T, GROUP)
    return out * jax.nn.silu(gate.astype(jnp.float32))


def conformer_conv_mixer(val, glu_gate, gate, conv_w, conv_b, ln_g, ln_b):
    u = val * jax.nn.sigmoid(glu_gate)
    u = jnp.pad(u, ((0, 0), (CONV_WIDTH - 1, 0), (0, 0)))
    y = lax.conv_general_dilated(u, conv_w[:, None, :].astype(u.dtype), window_strides=(1,), padding='VALID',
                                 dimension_numbers=('NWC', 'WIO', 'NWC'), feature_group_count=GROUP)
    y = layer_norm_f32(y + conv_b, ln_g, ln_b, LN_EPS)
    y = jax.nn.silu(y)
    return y * jax.nn.silu(gate.astype(jnp.float32))


def hgrn2_mixer(q, f_pre, i, gate, lb, norm_g):
    B, T, _ = q.shape
    n_chunks = T // CHUNK
    q = jax.nn.silu(q.astype(jnp.float32))
    f = lb + (1.0 - lb) * jax.nn.sigmoid(f_pre.astype(jnp.float32))
    log_f = jnp.log(f)
    k = 1.0 - f
    v = i.astype(jnp.float32)

    def to_chunks(t):
        return t.reshape(B, n_chunks, CHUNK, N_HEADS, HEAD_DIM).transpose(1, 0, 3, 2, 4)

    qc, kc, vc = to_chunks(q), to_chunks(k), to_chunks(v)
    bc = jnp.cumsum(to_chunks(log_f), axis=3)
    causal = (jnp.arange(CHUNK)[:, None] >= jnp.arange(CHUNK)[None, :])[:, :, None]

    def chunk_step(S, inp):
        q_c, k_c, v_c, b_c = inp
        o_inter = jnp.einsum('bhtk,bhkv->bhtv', q_c * jnp.exp(b_c), S)
        rel = jnp.where(causal, b_c[:, :, :, None, :] - b_c[:, :, None, :, :], -jnp.inf)
        att = jnp.einsum('bhtk,bhsk,bhtsk->bhts', q_c, k_c, jnp.exp(rel))
        o_intra = jnp.einsum('bhts,bhsv->bhtv', att, v_c)
        b_last = b_c[:, :, -1:, :]
        S = jnp.exp(b_last[:, :, 0, :])[..., None] * S + jnp.einsum('bhsk,bhsv->bhkv', k_c * jnp.exp(b_last - b_c), v_c)
        return S, o_inter + o_intra

    S0 = jnp.zeros((B, N_HEADS, HEAD_DIM, HEAD_DIM), jnp.float32)
    _, o = lax.scan(chunk_step, S0, (qc, kc, vc, bc))
    o = o.transpose(1, 0, 3, 2, 4).reshape(B, T, N_HEADS, HEAD_DIM)
    o = o * lax.rsqrt(jnp.mean(o * o, axis=-1, keepdims=True) + RMS_EPS) * norm_g.reshape(N_HEADS, HEAD_DIM)
    return o.reshape(B, T, GROUP) * jax.nn.silu(gate.astype(jnp.float32))


def stick_breaking_mixer(q, k, v, gate):
    B, T, _ = q.shape

    def heads(t):
        return t.astype(jnp.float32).reshape(B, T, N_HEADS, HEAD_DIM).transpose(0, 2, 1, 3)

    qh, kh, vh = heads(q), heads(k), heads(v)
    scale = HEAD_DIM ** -0.5
    outs = []
    for blk in range(T // SB_BLOCK):
        lo, hi = blk * SB_BLOCK, (blk + 1) * SB_BLOCK
        z = jnp.einsum('bhtd,bhsd->bhts', qh[:, :, lo:hi], kh[:, :, :hi]) * scale
        t_idx = lo + jnp.arange(SB_BLOCK)
        mask = jnp.arange(hi)[None, :] < t_idx[:, None]
        log_keep = jnp.where(mask, jax.nn.log_sigmoid(-z), 0.0)
        tail = lax.cumsum(log_keep, axis=3, reverse=True) - log_keep
        A = jnp.where(mask, jnp.exp(jax.nn.log_sigmoid(z) + tail), 0.0)
        outs.append(jnp.einsum('bhts,bhsd->bhtd', A, vh[:, :, :hi]))
    o = jnp.concatenate(outs, axis=2).transpose(0, 2, 1, 3).reshape(B, T, GROUP)
    return o * jax.nn.silu(gate.astype(jnp.float32))


def memory_cross_attention(h, mem_n, wq, wk, wv, wo):
    B, T, _ = h.shape
    M = mem_n.shape[1]
    q = (h @ wq).reshape(B, T, XATTN_HEADS, XATTN_HEAD_DIM)
    k = (mem_n @ wk).reshape(B, M, XATTN_HEADS, XATTN_HEAD_DIM)
    v = (mem_n @ wv).reshape(B, M, XATTN_HEADS, XATTN_HEAD_DIM)
    s = jnp.einsum('bthd,bmhd->bhtm', q, k).astype(jnp.float32) * (XATTN_HEAD_DIM ** -0.5)
    p = jax.nn.softmax(s, axis=-1).astype(h.dtype)
    o = jnp.einsum('bhtm,bmhd->bthd', p, v).reshape(B, T, D_MODEL)
    return o @ wo


def setup_inputs(seed: int = 0) -> dict:
    key = jax.random.key(seed)
    ks = jax.random.split(key, 32)
    f32 = jnp.float32
    nrm = lambda k, shape, s: jax.random.normal(k, shape, f32) * s
    gain = lambda k, shape: 1.0 + 0.02 * jax.random.normal(k, shape, f32)
    L = DEPTH
    return {
        'x': jax.random.normal(ks[0], (BATCH, SEQ, D_MODEL), f32),
        'mem': jax.random.normal(ks[1], (BATCH, N_MEM, D_MODEL), f32),
        'norm_mix': gain(ks[2], (L, D_MODEL)),
        'w_in': nrm(ks[3], (L, D_MODEL, N_IN), D_MODEL ** -0.5),
        'rwkv_mu': jax.random.uniform(ks[4], (L, N_RWKV_SHIFT), f32),
        'rwkv_w0': jax.random.uniform(ks[5], (L, GROUP), f32, minval=-6.0, maxval=1.0),
        'rwkv_w_up': nrm(ks[6], (L, RWKV_LORA, GROUP), 0.1),
        'rwkv_a0': nrm(ks[7], (L, GROUP), 0.1),
        'rwkv_a_up': nrm(ks[8], (L, RWKV_LORA, GROUP), 0.1),
        'rwkv_k_k': 0.85 + 0.05 * jax.random.normal(ks[9], (L, GROUP), f32),
        'rwkv_k_a': gain(ks[10], (L, GROUP)),
        'rwkv_r_k': nrm(ks[11], (L, N_HEADS, HEAD_DIM), 0.1),
        'rwkv_ln_g': gain(ks[12], (L, GROUP)),
        'rwkv_ln_b': nrm(ks[13], (L, GROUP), 0.02),
        'conv_w': nrm(ks[14], (L, CONV_WIDTH, GROUP), CONV_WIDTH ** -0.5),
        'conv_b': nrm(ks[15], (L, GROUP), 0.02),
        'conv_ln_g': gain(ks[16], (L, GROUP)),
        'conv_ln_b': nrm(ks[17], (L, GROUP), 0.02),
        'hgrn_lb_logits': nrm(ks[18], (L, GROUP), 0.5),
        'hgrn_norm_g': gain(ks[19], (L, GROUP)),
        'w_out': nrm(ks[20], (L, D_MIX, D_MODEL), D_MIX ** -0.5),
        'norm_xattn': gain(ks[21], (L, D_MODEL)),
        'norm_mem': gain(ks[22], (L, D_MODEL)),
        'xattn_wq': nrm(ks[23], (L, D_MODEL, D_MODEL), D_MODEL ** -0.5),
        'xattn_wk': nrm(ks[24], (L, D_MODEL, D_MODEL), D_MODEL ** -0.5),
        'xattn_wv': nrm(ks[25], (L, D_MODEL, D_MODEL), D_MODEL ** -0.5),
        'xattn_wo': nrm(ks[26], (L, D_MODEL, D_MODEL), D_MODEL ** -0.5),
        'norm_final': gain(ks[27], (D_MODEL,)),
    }


def reference(x, mem, norm_mix, w_in, rwkv_mu, rwkv_w0, rwkv_w_up, rwkv_a0, rwkv_a_up, rwkv_k_k, rwkv_k_a,
              rwkv_r_k, rwkv_ln_g, rwkv_ln_b, conv_w, conv_b, conv_ln_g, conv_ln_b, hgrn_lb_logits, hgrn_norm_g,
              w_out, norm_xattn, norm_mem, xattn_wq, xattn_wk, xattn_wv, xattn_wo, norm_final):
    lb_soft = jax.nn.softmax(hgrn_lb_logits.astype(jnp.float32), axis=0)
    lower_bounds = jnp.cumsum(lb_soft, axis=0) - lb_soft[0]
    splits = [N_RWKV, N_RWKV + N_CONV, N_RWKV + N_CONV + N_HGRN]
    for l in range(DEPTH):
        h = rms_norm(x, norm_mix[l])
        p = h @ w_in[l]
        p_a, p_b, p_c, p_d = jnp.split(p, splits, axis=-1)
        y_a = rwkv7_mixer(p_a[..., :N_RWKV_SHIFT], p_a[..., N_RWKV_SHIFT:], rwkv_mu[l], rwkv_w0[l], rwkv_w_up[l],
                          rwkv_a0[l], rwkv_a_up[l], rwkv_k_k[l], rwkv_k_a[l], rwkv_r_k[l], rwkv_ln_g[l], rwkv_ln_b[l])
        c_val, c_glu, c_gate = jnp.split(p_b, 3, axis=-1)
        y_b = conformer_conv_mixer(c_val, c_glu, c_gate, conv_w[l], conv_b[l], conv_ln_g[l], conv_ln_b[l])
        g_q, g_f, g_i, g_gate = jnp.split(p_c, 4, axis=-1)
        y_c = hgrn2_mixer(g_q, g_f, g_i, g_gate, lower_bounds[l], hgrn_norm_g[l])
        s_q, s_k, s_v, s_gate = jnp.split(p_d, 4, axis=-1)
        y_d = stick_breaking_mixer(s_q, s_k, s_v, s_gate)
        y = jnp.concatenate([y_a, y_b, y_c, y_d], axis=-1).astype(x.dtype)
        x = x + y @ w_out[l]
        hx = rms_norm(x, norm_xattn[l])
        mn = rms_norm(mem, norm_mem[l])
        x = x + memory_cross_attention(hx, mn, xattn_wq[l], xattn_wk[l], xattn_wv[l], xattn_wo[l])
    return rms_norm(x, norm_final)
```

```python
import functools

import jax
import jax.numpy as jnp
from jax import lax
from jax.experimental import pallas as pl
from jax.experimental.pallas import tpu as pltpu

F32 = jnp.float32
BF16 = jnp.bfloat16
HIGHEST = lax.Precision.HIGHEST

GROUP = 256
N_HEADS = 4
HEAD_DIM = GROUP // N_HEADS
RWKV_LORA = 32
CONV_WIDTH = 31
XATTN_HEADS = 4
RMS_EPS = 1e-6
LN_EPS = 1e-5
RWKV_LN_EPS = 64e-5

RWKV_SHIFT_COLS = 3 * GROUP + 128
RWKV_COLS = RWKV_SHIFT_COLS + GROUP

CHUNK = 64
SB_BLOCK = 128
CONV_TILE = 512
CONV_HALO = 32
ROW_TILE = 256
SB_LOG_UNDERFLOW = -120.0
NEG_BIG = -1e30
VMEM_LIMIT = 56 * 1024 * 1024


def _dot(a, b, precision=None):
    return jnp.dot(a, b, preferred_element_type=F32, precision=precision)


def _dot_nt(a, b, precision=None):
    return lax.dot_general(a, b, (((1,), (1,)), ((), ())), preferred_element_type=F32, precision=precision)


def _dot_tn(a, b, precision=None):
    return lax.dot_general(a, b, (((0,), (0,)), ((), ())), preferred_element_type=F32, precision=precision)


def _dot_split3(x, w_bf16):
    x1 = x.astype(BF16)
    r1 = x - x1.astype(F32)
    x2 = r1.astype(BF16)
    x3 = (r1 - x2.astype(F32)).astype(BF16)
    return _dot(x1, w_bf16) + _dot(x2, w_bf16) + _dot(x3, w_bf16)


def _iota2(shape, dim):
    return lax.broadcasted_iota(jnp.int32, shape, dim)


def _same_head_ones(n=GROUP):
    r = _iota2((n, n), 0) >> 6
    c = _iota2((n, n), 1) >> 6
    return jnp.where(r == c, 1.0, 0.0).astype(F32)


def _stack_heads(x):
    head = _iota2(x.shape, 1) >> 6
    return jnp.concatenate([jnp.where(head == h, x, 0.0) for h in range(N_HEADS)], axis=0)


def _unstack_heads(y, t):
    return y[0:t] + y[t:2 * t] + y[2 * t:3 * t] + y[3 * t:4 * t]


def _sigmoid(x):
    return 1.0 / (1.0 + jnp.exp(-x))


def _silu(x):
    return x * _sigmoid(x)


def _softplus(x):
    return jnp.maximum(x, 0.0) + jnp.log(1.0 + jnp.exp(-jnp.abs(x)))


def _inproj_kernel(x_ref, g_ref, wa_ref, wb_ref, wc_ref, wd_ref, pa_ref, pb_ref, pc_ref, pd_ref):
    x = x_ref[...]
    h = x * lax.rsqrt(jnp.mean(x * x, axis=-1, keepdims=True) + RMS_EPS) * g_ref[...]
    hb = h.astype(BF16)
    pa_ref[...] = _dot(hb, wa_ref[...])
    pb_ref[...] = _dot(hb, wb_ref[...])
    pc_ref[...] = _dot(hb, wc_ref[...])
    pd_ref[...] = _dot(hb, wd_ref[...])


def _inproj(x2d, g, wa, wb, wc, wd):
    n, d = x2d.shape
    tm = ROW_TILE
    row = lambda i: (i, 0)
    fixed = lambda i: (0, 0)
    widths = (wa.shape[1], wb.shape[1], wc.shape[1], wd.shape[1])
    return pl.pallas_call(
        _inproj_kernel,
        grid=(n // tm,),
        in_specs=[pl.BlockSpec((tm, d), row), pl.BlockSpec((1, d), fixed)]
        + [pl.BlockSpec((d, w), fixed) for w in widths],
        out_specs=[pl.BlockSpec((tm, w), row) for w in widths],
        out_shape=[jax.ShapeDtypeStruct((n, w), F32) for w in widths],
        compiler_params=pltpu.CompilerParams(dimension_semantics=("arbitrary",), vmem_limit_bytes=VMEM_LIMIT),
        name="inproj",
    )(x2d, g, wa, wb, wc, wd)


def _rwkv_kernel(p_ref, mu_ref, vec_ref, wup_ref, aup_ref, o_ref, st_ref, prev_ref):
    c = CHUNK
    n = N_HEADS * c

    @pl.when(pl.program_id(1) == 0)
    def _():
        st_ref[...] = jnp.zeros_like(st_ref)
        prev_ref[...] = jnp.zeros_like(prev_ref)

    p = p_ref[:, 0:RWKV_SHIFT_COLS]
    gate = p_ref[:, RWKV_SHIFT_COLS:RWKV_COLS]
    row = _iota2(p.shape, 0)
    p_prev = jnp.where(row == 0, prev_ref[0:1, :], pltpu.roll(p, 1, 0))
    prev_ref[0:1, :] = p[c - 1:c, :]
    m = p + (p_prev - p) * mu_ref[...]
    r = m[:, 0:GROUP]
    k = m[:, GROUP:2 * GROUP]
    v = m[:, 2 * GROUP:3 * GROUP]
    lora = m[:, 3 * GROUP:RWKV_SHIFT_COLS]
    w0, a0, k_k, k_a = vec_ref[0:1, :], vec_ref[1:2, :], vec_ref[2:3, :], vec_ref[3:4, :]
    r_k, ln_g, ln_b = vec_ref[4:5, :], vec_ref[5:6, :], vec_ref[6:7, :]

    gseg = _same_head_ones()
    w_log = -_softplus(-(w0 + _dot(jnp.tanh(lora), wup_ref[...], HIGHEST))) - 0.5
    lw = -jnp.exp(w_log)
    a_sig = _sigmoid(a0 + _dot(lora, aup_ref[...], HIGHEST))
    kk = k * k_k
    kk = kk / jnp.maximum(jnp.sqrt(_dot(kk * kk, gseg, HIGHEST)), 1e-12)
    k2 = k * (1.0 + (a_sig - 1.0) * k_a)
    bonus = _dot(r * k2 * r_k, gseg, HIGHEST) * v

    ti = _iota2((c, c), 0)
    si = _iota2((c, c), 1)
    ltri = jnp.where(si <= ti, 1.0, 0.0).astype(F32)
    cum = _dot(ltri, lw, HIGHEST)
    cum_last = cum[c - 1:c, :]
    g_inv = jnp.exp(-cum)
    g_rem = jnp.exp(cum_last - cum)
    a_t = _stack_heads(-kk * jnp.exp(cum - lw))
    r_t = _stack_heads(r * jnp.exp(cum))
    b_t = _stack_heads(kk * a_sig * g_inv)
    k_t = _stack_heads(k2 * g_inv)
    b_h = _stack_heads(kk * a_sig * g_rem)
    k_h = _stack_heads(k2 * g_rem)
    v_s = _stack_heads(v)

    ri = _iota2((n, n), 0)
    ci = _iota2((n, n), 1)
    same = (ri >> 6) == (ci >> 6)
    strict = same & (ci < ri)
    incl = same & (ci <= ri)
    a_ab = jnp.where(strict, _dot_nt(a_t, b_t, HIGHEST), 0.0)
    a_ak = jnp.where(strict, _dot_nt(a_t, k_t, HIGHEST), 0.0)
    a_rb = jnp.where(incl, _dot_nt(r_t, b_t, HIGHEST), 0.0)
    a_rk = jnp.where(incl, _dot_nt(r_t, k_t, HIGHEST), 0.0)

    eye = jnp.where(ri == ci, 1.0, 0.0).astype(F32)
    tinv = eye + jnp.where((ri >> 1) == (ci >> 1), a_ab, 0.0)
    for s in range(2, 7):
        e_b = jnp.where(((ri >> s) == (ci >> s)) & ((ri >> (s - 1)) != (ci >> (s - 1))), a_ab, 0.0)
        tinv = tinv + _dot(tinv, _dot(e_b, tinv, HIGHEST), HIGHEST)

    w_m = _dot(tinv, a_t, HIGHEST)
    z_m = _dot(tinv, _dot(a_ak, v_s, HIGHEST), HIGHEST)
    st = st_ref[...]
    u = _dot(w_m, st, HIGHEST) + z_m
    y_st = _dot(r_t, st, HIGHEST) + _dot(a_rb, u, HIGHEST) + _dot(a_rk, v_s, HIGHEST)
    dg = eye * jnp.exp(cum_last)
    st_ref[...] = _dot(dg, st, HIGHEST) + _dot_tn(b_h, u, HIGHEST) + _dot_tn(k_h, v_s, HIGHEST)

    y = _unstack_heads(y_st, c)
    mean = _dot(y, gseg, HIGHEST) * (1.0 / HEAD_DIM)
    yc = y - mean
    var = _dot(yc * yc, gseg, HIGHEST) * (1.0 / HEAD_DIM)
    yn = yc * lax.rsqrt(var + RWKV_LN_EPS) * ln_g + ln_b
    o_ref[...] = ((yn + bonus) * _silu(gate)).astype(o_ref.dtype)


def _rwkv(pa, mu, vec, wup, aup):
    b, t, _ = pa.shape
    fixed = lambda i, j: (0, 0)
    return pl.pallas_call(
        _rwkv_kernel,
        grid=(b, t // CHUNK),
        in_specs=[pl.BlockSpec((None, CHUNK, RWKV_COLS), lambda i, j: (i, j, 0)),
                  pl.BlockSpec((1, RWKV_SHIFT_COLS), fixed),
                  pl.BlockSpec((8, GROUP), fixed),
                  pl.BlockSpec((128, GROUP), fixed),
                  pl.BlockSpec((128, GROUP), fixed)],
        out_specs=pl.BlockSpec((None, CHUNK, GROUP), lambda i, j: (i, j, 0)),
        out_shape=jax.ShapeDtypeStruct((b, t, GROUP), BF16),
        scratch_shapes=[pltpu.VMEM((GROUP, GROUP), F32), pltpu.VMEM((8, RWKV_SHIFT_COLS), F32)],
        compiler_params=pltpu.CompilerParams(dimension_semantics=("arbitrary", "arbitrary"),
                                             vmem_limit_bytes=VMEM_LIMIT),
        name="rwkv7",
    )(pa, mu, vec, wup, aup)


def _conv_kernel(val_ref, glu_ref, gate_ref, w_ref, vec_ref, o_ref, u_ref):
    tt = CONV_TILE

    @pl.when(pl.program_id(1) == 0)
    def _():
        u_ref[0:CONV_HALO, :] = jnp.zeros((CONV_HALO, GROUP), F32)

    u_ref[CONV_HALO:CONV_HALO + tt, :] = val_ref[...] * _sigmoid(glu_ref[...])
    base = CONV_HALO - (CONV_WIDTH - 1)
    acc = jnp.zeros((tt, GROUP), F32)
    for j in range(CONV_WIDTH):
        acc = acc + u_ref[base + j:base + j + tt, :] * w_ref[j:j + 1, :]
    u_ref[0:CONV_HALO, :] = u_ref[tt:tt + CONV_HALO, :]
    y = acc + vec_ref[0:1, :]
    mu = jnp.mean(y, axis=-1, keepdims=True)
    yc = y - mu
    var = jnp.mean(yc * yc, axis=-1, keepdims=True)
    yn = yc * lax.rsqrt(var + LN_EPS) * vec_ref[1:2, :] + vec_ref[2:3, :]
    o_ref[...] = (_silu(yn) * _silu(gate_ref[...])).astype(o_ref.dtype)


def _conv(pb, w, vec):
    b, t, _ = pb.shape
    tt = CONV_TILE
    fixed = lambda i, j: (0, 0)
    col = lambda cidx: pl.BlockSpec((None, tt, GROUP), lambda i, j: (i, j, cidx))
    return pl.pallas_call(
        _conv_kernel,
        grid=(b, t // tt),
        in_specs=[col(0), col(1), col(2), pl.BlockSpec((32, GROUP), fixed), pl.BlockSpec((8, GROUP), fixed)],
        out_specs=pl.BlockSpec((None, tt, GROUP), lambda i, j: (i, j, 0)),
        out_shape=jax.ShapeDtypeStruct((b, t, GROUP), BF16),
        scratch_shapes=[pltpu.VMEM((CONV_HALO + tt, GROUP), F32)],
        compiler_params=pltpu.CompilerParams(dimension_semantics=("arbitrary", "arbitrary"),
                                             vmem_limit_bytes=VMEM_LIMIT),
        name="conformer_conv",
    )(pb, pb, pb, w, vec)


def _hgrn_kernel(q_ref, f_ref, i_ref, gate_ref, vec_ref, o_ref, s_ref, b_ref, k_ref, v_ref):
    c = CHUNK

    @pl.when(pl.program_id(1) == 0)
    def _():
        s_ref[...] = jnp.zeros_like(s_ref)

    lb, norm_g = vec_ref[0:1, :], vec_ref[1:2, :]
    q = _silu(q_ref[...])
    f = lb + (1.0 - lb) * _sigmoid(f_ref[...])
    kf = 1.0 - f
    v = i_ref[...]
    ti = _iota2((c, c), 0)
    si = _iota2((c, c), 1)
    ltri = jnp.where(si <= ti, 1.0, 0.0).astype(F32)
    bcum = _dot(ltri, jnp.log(f), HIGHEST)
    b_last = bcum[c - 1:c, :]
    b_ref[...] = bcum
    k_ref[...] = kf
    v_ref[...] = v

    gseg_f32 = _same_head_ones()
    gseg = gseg_f32.astype(BF16)
    s_old = s_ref[...]
    o_inter = _dot(q * jnp.exp(bcum), s_old, HIGHEST)

    trow = _iota2((c, GROUP), 0)

    def intra(s, acc):
        b_s = b_ref[pl.ds(s, 1), :]
        k_s = k_ref[pl.ds(s, 1), :]
        v_s = v_ref[pl.ds(s, 1), :]
        prod = q * jnp.exp(jnp.where(trow >= s, bcum - b_s, NEG_BIG)) * k_s
        p_hi = prod.astype(BF16)
        p_lo = (prod - p_hi.astype(F32)).astype(BF16)
        att = _dot(p_hi, gseg) + _dot(p_lo, gseg)
        return acc + att * v_s

    o_intra = lax.fori_loop(0, c, intra, jnp.zeros((c, GROUP), F32))

    n = GROUP
    ri = _iota2((n, n), 0)
    ci = _iota2((n, n), 1)
    eye = jnp.where(ri == ci, 1.0, 0.0).astype(F32)
    kv = _dot_tn(kf * jnp.exp(b_last - bcum), v, HIGHEST)
    s_ref[...] = _dot(eye * jnp.exp(b_last), s_old, HIGHEST) + jnp.where((ri >> 6) == (ci >> 6), kv, 0.0)

    o = o_inter + o_intra
    ms = _dot(o * o, gseg_f32, HIGHEST) * (1.0 / HEAD_DIM)
    o = o * lax.rsqrt(ms + RMS_EPS) * norm_g
    o_ref[...] = (o * _silu(gate_ref[...])).astype(o_ref.dtype)


def _hgrn(pc, vec):
    b, t, _ = pc.shape
    fixed = lambda i, j: (0, 0)
    col = lambda cidx: pl.BlockSpec((None, CHUNK, GROUP), lambda i, j: (i, j, cidx))
    return pl.pallas_call(
        _hgrn_kernel,
        grid=(b, t // CHUNK),
        in_specs=[col(0), col(1), col(2), col(3), pl.BlockSpec((8, GROUP), fixed)],
        out_specs=pl.BlockSpec((None, CHUNK, GROUP), lambda i, j: (i, j, 0)),
        out_shape=jax.ShapeDtypeStruct((b, t, GROUP), BF16),
        scratch_shapes=[pltpu.VMEM((GROUP, GROUP), F32)] + [pltpu.VMEM((CHUNK, GROUP), F32)] * 3,
        compiler_params=pltpu.CompilerParams(dimension_semantics=("arbitrary", "arbitrary"),
                                             vmem_limit_bytes=VMEM_LIMIT),
        name="hgrn2",
    )(pc, pc, pc, pc, vec)


def _sb_kernel(q_ref, k_ref, v_ref, gate_ref, o_ref, acc_ref, carry_ref):
    tq = SB_BLOCK
    n = N_HEADS * tq
    qb = pl.program_id(1)
    qs = _stack_heads(q_ref[...] * (HEAD_DIM ** -0.5))
    acc_ref[...] = jnp.zeros_like(acc_ref)
    carry_ref[...] = jnp.zeros_like(carry_ref)

    row = _iota2((n, tq), 0) & (tq - 1)
    col = _iota2((n, tq), 1)
    si = _iota2((tq, 2 * tq), 0)
    ci = _iota2((tq, 2 * tq), 1)
    cs_mat = jnp.where((ci >= tq) | (si > ci), 1.0, 0.0).astype(BF16)
    head = _iota2((n, GROUP), 1) >> 6
    own = head == (_iota2((n, GROUP), 0) >> 7)

    def body(state):
        j, _ = state
        start = pl.multiple_of(j * tq, tq)
        kb = k_ref[pl.ds(start, tq), :]
        vb = v_ref[pl.ds(start, tq), :]
        z = _dot_nt(qs, kb, HIGHEST)
        log_keep = -_softplus(z)
        log_beta = z + log_keep
        mask = col < row + (qb - j) * tq
        cs = _dot_split3(jnp.where(mask, log_keep, 0.0), cs_mat)
        carry = carry_ref[...]
        att = jnp.where(mask, jnp.exp(log_beta + cs[:, 0:tq] + carry), 0.0)
        carry = carry + cs[:, tq:2 * tq]
        carry_ref[...] = carry
        av = jnp.where(own, _dot(att, vb, HIGHEST), 0.0)
        acc_ref[...] += _unstack_heads(av, tq)
        more = (j > 0) & (jnp.max(carry) > SB_LOG_UNDERFLOW)
        return j - 1, more.astype(jnp.int32)

    lax.while_loop(lambda s: s[1] > 0, body, (qb, jnp.int32(1)))
    o_ref[...] = (acc_ref[...] * _silu(gate_ref[...])).astype(o_ref.dtype)


def _sb(pd):
    b, t, _ = pd.shape
    tq = SB_BLOCK
    blk = lambda cidx: pl.BlockSpec((None, tq, GROUP), lambda i, j: (i, j, cidx))
    full = lambda cidx: pl.BlockSpec((None, t, GROUP), lambda i, j: (i, 0, cidx))
    return pl.pallas_call(
        _sb_kernel,
        grid=(b, t // tq),
        in_specs=[blk(0), full(1), full(2), blk(3)],
        out_specs=pl.BlockSpec((None, tq, GROUP), lambda i, j: (i, j, 0)),
        out_shape=jax.ShapeDtypeStruct((b, t, GROUP), BF16),
        scratch_shapes=[pltpu.VMEM((tq, GROUP), F32), pltpu.VMEM((N_HEADS * tq, tq), F32)],
        compiler_params=pltpu.CompilerParams(dimension_semantics=("arbitrary", "arbitrary"),
                                             vmem_limit_bytes=VMEM_LIMIT),
        name="stick_breaking",
    )(pd, pd, pd, pd)


def _memkv_kernel(m_ref, g_ref, wk_ref, wv_ref, k_ref, v_ref):
    x = m_ref[...]
    h = (x * lax.rsqrt(jnp.mean(x * x, axis=-1, keepdims=True) + RMS_EPS) * g_ref[...]).astype(BF16)
    k_ref[...] = _dot(h, wk_ref[...]).astype(k_ref.dtype)
    v_ref[...] = _dot(h, wv_ref[...]).astype(v_ref.dtype)


def _memkv(mem2d, g, wk, wv):
    n, d = mem2d.shape
    tm = ROW_TILE
    row = lambda i: (i, 0)
    fixed = lambda i: (0, 0)
    return pl.pallas_call(
        _memkv_kernel,
        grid=(n // tm,),
        in_specs=[pl.BlockSpec((tm, d), row), pl.BlockSpec((1, d), fixed),
                  pl.BlockSpec((d, d), fixed), pl.BlockSpec((d, d), fixed)],
        out_specs=[pl.BlockSpec((tm, d), row)] * 2,
        out_shape=[jax.ShapeDtypeStruct((n, d), BF16)] * 2,
        compiler_params=pltpu.CompilerParams(dimension_semantics=("arbitrary",), vmem_limit_bytes=VMEM_LIMIT),
        name="mem_kv",
    )(mem2d, g, wk, wv)


def _post_kernel(x_ref, ya_ref, yb_ref, yc_ref, yd_ref, wout_ref, gx_ref, wq_ref, k_ref, v_ref, wo_ref,
                 gf_ref, o_ref, *, final):
    d = x_ref.shape[-1]
    hd = d // XATTN_HEADS
    x1 = x_ref[...]
    for g, y_ref in enumerate((ya_ref, yb_ref, yc_ref, yd_ref)):
        x1 = x1 + _dot(y_ref[...], wout_ref[g * GROUP:(g + 1) * GROUP, :])
    hx = (x1 * lax.rsqrt(jnp.mean(x1 * x1, axis=-1, keepdims=True) + RMS_EPS) * gx_ref[...]).astype(BF16)
    q = _dot(hx, wq_ref[...])
    outs = []
    for h in range(XATTN_HEADS):
        sl = slice(h * hd, (h + 1) * hd)
        s = _dot_nt(q[:, sl].astype(BF16), k_ref[:, sl]) * (hd ** -0.5)
        s = s - jnp.max(s, axis=-1, keepdims=True)
        e = jnp.exp(s)
        p = e / jnp.sum(e, axis=-1, keepdims=True)
        outs.append(_dot(p.astype(BF16), v_ref[:, sl]).astype(BF16))
    x2 = x1 + _dot(jnp.concatenate(outs, axis=-1), wo_ref[...])
    if final:
        x2 = x2 * lax.rsqrt(jnp.mean(x2 * x2, axis=-1, keepdims=True) + RMS_EPS) * gf_ref[...]
    o_ref[...] = x2


def _post(x, ys, wout, gx, wq, kmem, vmem, wo, gf, final):
    b, t, d = x.shape
    m = kmem.shape[1]
    tm = ROW_TILE
    fixed = lambda i, j: (0, 0)
    rowblk = lambda w: pl.BlockSpec((None, tm, w), lambda i, j: (i, j, 0))
    return pl.pallas_call(
        functools.partial(_post_kernel, final=final),
        grid=(b, t // tm),
        in_specs=[rowblk(d)] + [rowblk(GROUP)] * 4
        + [pl.BlockSpec((d, d), fixed), pl.BlockSpec((1, d), fixed), pl.BlockSpec((d, d), fixed),
           pl.BlockSpec((None, m, d), lambda i, j: (i, 0, 0)), pl.BlockSpec((None, m, d), lambda i, j: (i, 0, 0)),
           pl.BlockSpec((d, d), fixed), pl.BlockSpec((1, d), fixed)],
        out_specs=rowblk(d),
        out_shape=jax.ShapeDtypeStruct((b, t, d), F32),
        compiler_params=pltpu.CompilerParams(dimension_semantics=("arbitrary", "arbitrary"),
                                             vmem_limit_bytes=VMEM_LIMIT),
        name="outproj_xattn",
    )(x, *ys, wout, gx, wq, kmem, vmem, wo, gf)


def _rows8(*rows):
    rows = [r.reshape(1, -1).astype(F32) for r in rows]
    width = rows[0].shape[1]
    return jnp.concatenate(rows + [jnp.zeros((8 - len(rows), width), F32)], axis=0)


def kernel(x, mem, norm_mix, w_in, rwkv_mu, rwkv_w0, rwkv_w_up, rwkv_a0, rwkv_a_up, rwkv_k_k, rwkv_k_a,
           rwkv_r_k, rwkv_ln_g, rwkv_ln_b, conv_w, conv_b, conv_ln_g, conv_ln_b, hgrn_lb_logits, hgrn_norm_g,
           w_out, norm_xattn, norm_mem, xattn_wq, xattn_wk, xattn_wv, xattn_wo, norm_final):
    b, t, d = x.shape
    depth = w_in.shape[0]
    n_shift = 3 * GROUP + 2 * RWKV_LORA
    n_rwkv = n_shift + GROUP
    lb_soft = jax.nn.softmax(hgrn_lb_logits.astype(F32), axis=0)
    lower_bounds = jnp.cumsum(lb_soft, axis=0) - lb_soft[0]
    mem2d = mem.reshape(-1, d)
    pad_lora = RWKV_SHIFT_COLS - n_shift
    for l in range(depth):
        w = w_in[l]
        wa = jnp.concatenate([w[:, :n_shift], jnp.zeros((d, pad_lora), F32), w[:, n_shift:n_rwkv]], axis=1)
        wb = w[:, n_rwkv:n_rwkv + 3 * GROUP]
        wc = w[:, n_rwkv + 3 * GROUP:n_rwkv + 7 * GROUP]
        wd = w[:, n_rwkv + 7 * GROUP:]
        pa, pb, pc, pd = _inproj(x.reshape(-1, d), norm_mix[l].reshape(1, d), wa.astype(BF16), wb.astype(BF16),
                                 wc.astype(BF16), wd.astype(BF16))
        pa = pa.reshape(b, t, -1)
        pb = pb.reshape(b, t, -1)
        pc = pc.reshape(b, t, -1)
        pd = pd.reshape(b, t, -1)

        mu = jnp.concatenate([rwkv_mu[l], jnp.zeros((pad_lora,), F32)]).reshape(1, -1)
        rvec = _rows8(rwkv_w0[l], rwkv_a0[l], rwkv_k_k[l], rwkv_k_a[l], rwkv_r_k[l], rwkv_ln_g[l], rwkv_ln_b[l])
        lora_rows = 128 - 2 * RWKV_LORA
        wup = jnp.concatenate([rwkv_w_up[l], jnp.zeros((RWKV_LORA + lora_rows, GROUP), F32)], axis=0)
        aup = jnp.concatenate([jnp.zeros((RWKV_LORA, GROUP), F32), rwkv_a_up[l],
                               jnp.zeros((lora_rows, GROUP), F32)], axis=0)
        y_a = _rwkv(pa, mu, rvec, wup, aup)

        cw = jnp.concatenate([conv_w[l], jnp.zeros((32 - CONV_WIDTH, GROUP), F32)], axis=0)
        y_b = _conv(pb, cw, _rows8(conv_b[l], conv_ln_g[l], conv_ln_b[l]))
        y_c = _hgrn(pc, _rows8(lower_bounds[l], hgrn_norm_g[l]))
        y_d = _sb(pd)

        kmem, vmem = _memkv(mem2d, norm_mem[l].reshape(1, d), xattn_wk[l].astype(BF16), xattn_wv[l].astype(BF16))
        x = _post(x, (y_a, y_b, y_c, y_d), w_out[l].astype(BF16), norm_xattn[l].reshape(1, d),
                  xattn_wq[l].astype(BF16), kmem.reshape(b, -1, d), vmem.reshape(b, -1, d),
                  xattn_wo[l].astype(BF16), norm_final.reshape(1, d), final=(l == depth - 1))
    return x
```

```python
import functools

import jax
import jax.numpy as jnp
from jax import lax
from jax.experimental import pallas as pl
from jax.experimental.pallas import tpu as pltpu

F32 = jnp.float32
BF16 = jnp.bfloat16

GROUP = 256
N_HEADS = 4
HEAD_DIM = GROUP // N_HEADS
RWKV_LORA = 32
CONV_WIDTH = 31
XATTN_HEADS = 4
RMS_EPS = 1e-6
LN_EPS = 1e-5
RWKV_LN_EPS = 64e-5

LANES = 128
RWKV_SHIFT_COLS = 3 * GROUP + LANES
RWKV_COLS = RWKV_SHIFT_COLS + GROUP

CHUNK = 64
RWKV_TILE = 2 * CHUNK
SUB = 16
SB_BLOCK = 128
SB_QBLOCKS = 2
CONV_TILE = 512
CONV_HALO = 32
ROW_TILE = 256
SB_LOG_UNDERFLOW = -120.0
NEG_BIG = -1e30
VMEM_LIMIT = 56 * 1024 * 1024

NN = ((1,), (0,))
NT = ((1,), (1,))
TN = ((0,), (0,))


def _pieces(x, n):
    out = []
    for _ in range(n - 1):
        p = x.astype(BF16)
        out.append(p)
        x = x - p.astype(F32)
    out.append(x.astype(BF16))
    return out


def _dg(a, b, dims=NN):
    return lax.dot_general(a, b, (dims, ((), ())), preferred_element_type=F32)


def _mm(a, b, dims=NN, passes=1):
    if passes == 1:
        return _dg(a.astype(BF16), b.astype(BF16), dims)
    a_hi, a_lo = _pieces(a, 2)
    b_hi, b_lo = _pieces(b, 2)
    return _dg(a_hi, b_hi, dims) + (_dg(a_hi, b_lo, dims) + _dg(a_lo, b_hi, dims))


def _mm_exact_rhs(x, w_bf16, n=3):
    return functools.reduce(lambda s, t: s + t, [_dg(p, w_bf16) for p in _pieces(x, n)])


def _mm_exact_lhs(w_bf16, x, n=3):
    return functools.reduce(lambda s, t: s + t, [_dg(w_bf16, p) for p in _pieces(x, n)])


def _iota2(shape, dim):
    return lax.broadcasted_iota(jnp.int32, shape, dim)


def _log2(n):
    assert n & (n - 1) == 0
    return n.bit_length() - 1


HEAD_SHIFT = _log2(HEAD_DIM)


def _same_head_ones(n=GROUP):
    r = _iota2((n, n), 0) >> HEAD_SHIFT
    c = _iota2((n, n), 1) >> HEAD_SHIFT
    return jnp.where(r == c, 1.0, 0.0).astype(BF16)


def _lower_ones(n):
    return jnp.where(_iota2((n, n), 1) <= _iota2((n, n), 0), 1.0, 0.0).astype(BF16)


def _stack_heads(x):
    head = _iota2(x.shape, 1) >> HEAD_SHIFT
    return jnp.concatenate([jnp.where(head == h, x, 0.0) for h in range(N_HEADS)], axis=0)


def _unstack_heads(y, t):
    return functools.reduce(lambda s, u: s + u, [y[h * t:(h + 1) * t] for h in range(N_HEADS)])


def _own_head_lanes(t):
    return (_iota2((N_HEADS * t, GROUP), 1) >> HEAD_SHIFT) == (_iota2((N_HEADS * t, GROUP), 0) >> _log2(t))


def _sigmoid(x):
    return 1.0 / (1.0 + jnp.exp(-x))


def _silu(x):
    return x * _sigmoid(x)


def _softplus(x):
    return jnp.maximum(x, 0.0) + jnp.log(1.0 + jnp.exp(-jnp.abs(x)))


def _rms_scale(x):
    return x * lax.rsqrt(jnp.mean(x * x, axis=-1, keepdims=True) + RMS_EPS)


def _inproj_kernel(x_ref, g_ref, wa_ref, wb_ref, wc_ref, wd_ref, pa_ref, pb_ref, pc_ref, pd_ref):
    hb = (_rms_scale(x_ref[...]) * g_ref[...]).astype(BF16)
    pa_ref[...] = _dg(hb, wa_ref[...])
    pb_ref[...] = _dg(hb, wb_ref[...])
    pc_ref[...] = _dg(hb, wc_ref[...])
    pd_ref[...] = _dg(hb, wd_ref[...])


def _inproj(x2d, g, wa, wb, wc, wd):
    n, d = x2d.shape
    tm = ROW_TILE
    row = lambda i: (i, 0)
    fixed = lambda i: (0, 0)
    widths = (wa.shape[1], wb.shape[1], wc.shape[1], wd.shape[1])
    return pl.pallas_call(
        _inproj_kernel,
        grid=(n // tm,),
        in_specs=[pl.BlockSpec((tm, d), row), pl.BlockSpec((1, d), fixed)]
        + [pl.BlockSpec((d, w), fixed) for w in widths],
        out_specs=[pl.BlockSpec((tm, w), row) for w in widths],
        out_shape=[jax.ShapeDtypeStruct((n, w), F32) for w in widths],
        compiler_params=pltpu.CompilerParams(dimension_semantics=("arbitrary",), vmem_limit_bytes=VMEM_LIMIT),
        name="inproj",
    )(x2d, g, wa, wb, wc, wd)


def _rwkv_rows(p, prev_row, mu, vec, wup, aup):
    row = _iota2(p.shape, 0)
    p_prev = jnp.where(row == 0, prev_row, pltpu.roll(p, 1, 0))
    m = p + (p_prev - p) * mu
    r = m[:, 0:GROUP]
    k = m[:, GROUP:2 * GROUP]
    v = m[:, 2 * GROUP:3 * GROUP]
    lora = m[:, 3 * GROUP:RWKV_SHIFT_COLS]
    w0, a0, k_k, k_a, r_k = vec[0:1, :], vec[1:2, :], vec[2:3, :], vec[3:4, :], vec[4:5, :]
    gseg = _same_head_ones()
    w_log = -_softplus(-(w0 + _mm(jnp.tanh(lora), wup, NN, 3))) - 0.5
    lw = -jnp.exp(w_log)
    a_sig = _sigmoid(a0 + _mm(lora, aup, NN, 3))
    kk = k * k_k
    kk = kk / jnp.maximum(jnp.sqrt(_mm_exact_rhs(kk * kk, gseg)), 1e-12)
    k2 = k * (1.0 + (a_sig - 1.0) * k_a)
    bonus = _mm_exact_rhs(r * k2 * r_k, gseg) * v
    return r, k2, v, kk, a_sig, lw, bonus


def _rwkv_prep(chains):
    c = CHUNK
    n = N_HEADS * c
    ltri = _lower_ones(c)
    ri = _iota2((n, n), 0)
    ci = _iota2((n, n), 1)
    cshift = _log2(c)
    same = (ri >> cshift) == (ci >> cshift)
    strict = same & (ci < ri)
    incl = same & (ci <= ri)
    fs = []
    for r, k2, v, kk, a_sig, lw in chains:
        cum = _mm_exact_lhs(ltri, lw)
        cum_last = cum[c - 1:c, :]
        g_inv = jnp.exp(-cum)
        g_rem = jnp.exp(cum_last - cum)
        fs.append(dict(a_t=_stack_heads(-kk * jnp.exp(cum - lw)),
                       r_t=_stack_heads(r * jnp.exp(cum)),
                       b_t=_stack_heads(kk * a_sig * g_inv), k_t=_stack_heads(k2 * g_inv),
                       b_h=_stack_heads(kk * a_sig * g_rem), k_h=_stack_heads(k2 * g_rem),
                       v_s=_stack_heads(v), decay=jnp.exp(cum_last)))
    for f in fs:
        f["a_ab"] = jnp.where(strict, _mm(f["a_t"], f["b_t"], NT, 3), 0.0)
    for f in fs:
        f["a_ak"] = jnp.where(strict, _mm(f["a_t"], f["k_t"], NT), 0.0)
    for f in fs:
        f["a_rb"] = jnp.where(incl, _mm(f["r_t"], f["b_t"], NT), 0.0)
    for f in fs:
        f["a_rk"] = jnp.where(incl, _mm(f["r_t"], f["k_t"], NT), 0.0)

    eye = jnp.where(ri == ci, 1.0, 0.0)
    for f in fs:
        f["tinv"] = eye + jnp.where((ri >> 1) == (ci >> 1), f["a_ab"], 0.0)
    for s in range(2, cshift + 1):
        lower_left = ((ri >> s) == (ci >> s)) & ((ri >> (s - 1)) != (ci >> (s - 1)))
        xs = [_mm(jnp.where(lower_left, f["a_ab"], 0.0), f["tinv"]) for f in fs]
        xs = [_mm(f["tinv"], x) for f, x in zip(fs, xs)]
        for f, x in zip(fs, xs):
            f["tinv"] = f["tinv"] + x
    for f in fs:
        f["w_m"] = _mm(f["tinv"], f["a_t"])
    xs = [_mm(f["a_ak"], f["v_s"]) for f in fs]
    for f, x in zip(fs, xs):
        f["z_m"] = _mm(f["tinv"], x)
    for f in fs:
        f["y_v"] = _mm(f["a_rk"], f["v_s"])
    for f in fs:
        f["s_v"] = _mm(f["v_s"], f["k_h"], TN)
    return fs


def _rwkv_step(fs, states):
    us = [_mm(f["w_m"], s, NT) + f["z_m"] for f, s in zip(fs, states)]
    ys = [_mm(f["r_t"], s, NT) for f, s in zip(fs, states)]
    ys = [y + _mm(f["a_rb"], u) + f["y_v"] for f, u, y in zip(fs, us, ys)]
    new = [s * f["decay"] + _mm(u, f["b_h"], TN) + f["s_v"] for f, u, s in zip(fs, us, states)]
    return ys, new


def _rwkv_kernel(p_ref, mu_ref, vec_ref, wup_ref, aup_ref, o_ref, s_ref, prev_ref):
    @pl.when(pl.program_id(0) == 0)
    def _():
        s_ref[...] = jnp.zeros_like(s_ref)
        prev_ref[...] = jnp.zeros_like(prev_ref)

    c = CHUNK
    nb, tile = p_ref.shape[0], p_ref.shape[1]
    vec = vec_ref[...]
    ln_g, ln_b = vec[5:6, :], vec[6:7, :]
    gseg = _same_head_ones()
    nch = tile // c
    rows = []
    for bi in range(nb):
        p = p_ref[bi, :, 0:RWKV_SHIFT_COLS]
        rows.append(_rwkv_rows(p, prev_ref[bi, 0:1, :], mu_ref[...], vec, wup_ref[...], aup_ref[...]))
        prev_ref[bi, 0:1, :] = p[tile - 1:tile, :]
    factors = _rwkv_prep([tuple(a[j * c:(j + 1) * c] for a in rows[bi][:6]) for j in range(nch) for bi in range(nb)])
    states = [s_ref[bi] for bi in range(nb)]
    y_parts = [[] for _ in range(nb)]
    for j in range(nch):
        ys, states = _rwkv_step(factors[j * nb:(j + 1) * nb], states)
        for bi in range(nb):
            y_parts[bi].append(_unstack_heads(ys[bi], c))
    for bi in range(nb):
        s_ref[bi] = states[bi]
        y = jnp.concatenate(y_parts[bi], axis=0)
        mean = _mm_exact_rhs(y, gseg) * (1.0 / HEAD_DIM)
        yc = y - mean
        var = _mm_exact_rhs(yc * yc, gseg) * (1.0 / HEAD_DIM)
        yn = yc * lax.rsqrt(var + RWKV_LN_EPS) * ln_g + ln_b
        gate = p_ref[bi, :, RWKV_SHIFT_COLS:RWKV_COLS]
        o_ref[bi] = ((yn + rows[bi][6]) * _silu(gate)).astype(o_ref.dtype)


def _rwkv(pa, mu, vec, wup, aup):
    b, t, _ = pa.shape
    fixed = lambda j: (0, 0)
    return pl.pallas_call(
        _rwkv_kernel,
        grid=(t // RWKV_TILE,),
        in_specs=[pl.BlockSpec((b, RWKV_TILE, RWKV_COLS), lambda j: (0, j, 0)),
                  pl.BlockSpec((1, RWKV_SHIFT_COLS), fixed),
                  pl.BlockSpec((8, GROUP), fixed),
                  pl.BlockSpec((LANES, GROUP), fixed),
                  pl.BlockSpec((LANES, GROUP), fixed)],
        out_specs=pl.BlockSpec((b, RWKV_TILE, GROUP), lambda j: (0, j, 0)),
        out_shape=jax.ShapeDtypeStruct((b, t, GROUP), BF16),
        scratch_shapes=[pltpu.VMEM((b, GROUP, GROUP), F32), pltpu.VMEM((b, 8, RWKV_SHIFT_COLS), F32)],
        compiler_params=pltpu.CompilerParams(dimension_semantics=("arbitrary",), vmem_limit_bytes=VMEM_LIMIT),
        name="rwkv7",
    )(pa, mu, vec, wup, aup)


def _conv_kernel(val_ref, glu_ref, gate_ref, w_ref, vec_ref, o_ref, u_ref):
    tt = CONV_TILE

    @pl.when(pl.program_id(1) == 0)
    def _():
        u_ref[0:CONV_HALO, :] = jnp.zeros((CONV_HALO, GROUP), F32)

    u_ref[CONV_HALO:CONV_HALO + tt, :] = val_ref[...] * _sigmoid(glu_ref[...])
    base = CONV_HALO - (CONV_WIDTH - 1)
    acc = jnp.zeros((tt, GROUP), F32)
    for j in range(CONV_WIDTH):
        acc = acc + u_ref[base + j:base + j + tt, :] * w_ref[j:j + 1, :]
    u_ref[0:CONV_HALO, :] = u_ref[tt:tt + CONV_HALO, :]
    y = acc + vec_ref[0:1, :]
    mu = jnp.mean(y, axis=-1, keepdims=True)
    yc = y - mu
    var = jnp.mean(yc * yc, axis=-1, keepdims=True)
    yn = yc * lax.rsqrt(var + LN_EPS) * vec_ref[1:2, :] + vec_ref[2:3, :]
    o_ref[...] = (_silu(yn) * _silu(gate_ref[...])).astype(o_ref.dtype)


def _conv(pb, w, vec):
    b, t, _ = pb.shape
    tt = CONV_TILE
    fixed = lambda i, j: (0, 0)
    col = lambda cidx: pl.BlockSpec((None, tt, GROUP), lambda i, j: (i, j, cidx))
    return pl.pallas_call(
        _conv_kernel,
        grid=(b, t // tt),
        in_specs=[col(0), col(1), col(2), pl.BlockSpec((32, GROUP), fixed), pl.BlockSpec((8, GROUP), fixed)],
        out_specs=pl.BlockSpec((None, tt, GROUP), lambda i, j: (i, j, 0)),
        out_shape=jax.ShapeDtypeStruct((b, t, GROUP), BF16),
        scratch_shapes=[pltpu.VMEM((CONV_HALO + tt, GROUP), F32)],
        compiler_params=pltpu.CompilerParams(dimension_semantics=("arbitrary", "arbitrary"),
                                             vmem_limit_bytes=VMEM_LIMIT),
        name="conformer_conv",
    )(pb, pb, pb, w, vec)


def _hgrn_chunk(qp, fp, ip, gate, st_old, lb, norm_g, b_ref, k_ref, v_ref, phi_ref, plo_ref, r_ref):
    c = CHUNK
    nsub = c // SUB
    sshift = _log2(SUB)
    q = _silu(qp)
    f = lb + (1.0 - lb) * _sigmoid(fp)
    kf = 1.0 - f
    v = ip
    bcum = _mm_exact_lhs(_lower_ones(c), jnp.log(f))
    b_last = bcum[c - 1:c, :]
    b_ref[...] = bcum
    k_ref[...] = kf
    v_ref[...] = v

    gseg = _same_head_ones()
    o_inter = _mm(q * jnp.exp(bcum), st_old, NT, 3)
    st_new = st_old * jnp.exp(b_last) + jnp.where(
        (_iota2((GROUP, GROUP), 0) >> HEAD_SHIFT) == (_iota2((GROUP, GROUP), 1) >> HEAD_SHIFT),
        _mm(v, kf * jnp.exp(b_last - bcum), TN, 3), 0.0)

    trow = _iota2((c, GROUP), 0)
    tsub = trow >> sshift
    beta = b_ref[SUB - 1:SUB, :]
    for i in range(2, nsub):
        beta = jnp.where(tsub == i, b_ref[i * SUB - 1:i * SUB, :], beta)
    q_off = q * jnp.exp(jnp.where(tsub >= 1, bcum - beta, NEG_BIG))
    k_cat = jnp.concatenate(
        [kf * jnp.exp(jnp.where(trow < i * SUB, b_ref[i * SUB - 1:i * SUB, :] - bcum, NEG_BIG))
         for i in range(1, nsub)], axis=0)
    att = _mm(_stack_heads(q_off), k_cat, NT, 3)
    arow = (_iota2(att.shape, 0) & (c - 1)) >> sshift
    acol = _iota2(att.shape, 1) >> _log2(c)
    att = jnp.where(acol + 1 == arow, att, 0.0)
    o_off = _mm(att, jnp.concatenate([v] * (nsub - 1), axis=0))
    o_off = _unstack_heads(jnp.where(_own_head_lanes(c), o_off, 0.0), c)

    tau = _iota2((SUB, GROUP), 0)
    for i in range(nsub):
        qi = q[i * SUB:(i + 1) * SUB]
        bi = bcum[i * SUB:(i + 1) * SUB]
        for s in range(SUB):
            src = i * SUB + s
            prod = qi * jnp.exp(jnp.where(tau >= s, bi - b_ref[src:src + 1, :], NEG_BIG)) * k_ref[src:src + 1, :]
            hi = prod.astype(BF16)
            phi_ref[src * SUB:(src + 1) * SUB, :] = hi
            plo_ref[src * SUB:(src + 1) * SUB, :] = (prod - hi.astype(F32)).astype(BF16)
    r_ref[...] = _dg(phi_ref[...], gseg) + _dg(plo_ref[...], gseg)
    diag = []
    for i in range(nsub):
        acc = jnp.zeros((SUB, GROUP), F32)
        for s in range(SUB):
            src = i * SUB + s
            acc = acc + r_ref[src * SUB:(src + 1) * SUB, :] * v_ref[src:src + 1, :]
        diag.append(acc)
    o = o_inter + o_off + jnp.concatenate(diag, axis=0)

    ms = _mm_exact_rhs(o * o, gseg) * (1.0 / HEAD_DIM)
    o = o * lax.rsqrt(ms + RMS_EPS) * norm_g
    return o * _silu(gate), st_new


def _hgrn_kernel(p_ref, vec_ref, o_ref, s_ref, b_ref, k_ref, v_ref, phi_ref, plo_ref, r_ref):
    @pl.when(pl.program_id(0) == 0)
    def _():
        s_ref[...] = jnp.zeros_like(s_ref)

    for bi in range(p_ref.shape[0]):
        out, st_new = _hgrn_chunk(p_ref[bi, :, 0:GROUP], p_ref[bi, :, GROUP:2 * GROUP],
                                  p_ref[bi, :, 2 * GROUP:3 * GROUP], p_ref[bi, :, 3 * GROUP:4 * GROUP], s_ref[bi],
                                  vec_ref[0:1, :], vec_ref[1:2, :], b_ref.at[bi], k_ref.at[bi], v_ref.at[bi],
                                  phi_ref.at[bi], plo_ref.at[bi], r_ref.at[bi])
        s_ref[bi] = st_new
        o_ref[bi] = out.astype(o_ref.dtype)


def _hgrn(pc, vec):
    b, t, _ = pc.shape
    return pl.pallas_call(
        _hgrn_kernel,
        grid=(t // CHUNK,),
        in_specs=[pl.BlockSpec((b, CHUNK, 4 * GROUP), lambda j: (0, j, 0)), pl.BlockSpec((8, GROUP), lambda j: (0, 0))],
        out_specs=pl.BlockSpec((b, CHUNK, GROUP), lambda j: (0, j, 0)),
        out_shape=jax.ShapeDtypeStruct((b, t, GROUP), BF16),
        scratch_shapes=[pltpu.VMEM((b, GROUP, GROUP), F32)] + [pltpu.VMEM((b, CHUNK, GROUP), F32)] * 3
        + [pltpu.VMEM((b, CHUNK * SUB, GROUP), BF16)] * 2 + [pltpu.VMEM((b, CHUNK * SUB, GROUP), F32)],
        compiler_params=pltpu.CompilerParams(dimension_semantics=("arbitrary",), vmem_limit_bytes=VMEM_LIMIT),
        name="hgrn2",
    )(pc, vec)


def _sb_kernel(q_ref, k_ref, v_ref, gate_ref, o_ref, acc_ref, carry_ref):
    tq = SB_BLOCK
    nq = SB_QBLOCKS
    n = N_HEADS * tq
    first_qb = pl.program_id(1) * nq
    qs = [_stack_heads(q_ref[i * tq:(i + 1) * tq, :] * (HEAD_DIM ** -0.5)) for i in range(nq)]
    acc_ref[...] = jnp.zeros_like(acc_ref)
    carry_ref[...] = jnp.zeros_like(carry_ref)

    row = _iota2((n, tq), 0) & (tq - 1)
    col = _iota2((n, tq), 1)
    si = _iota2((tq, 2 * tq), 0)
    ci = _iota2((tq, 2 * tq), 1)
    cs_mat = jnp.where((ci >= tq) | (si > ci), 1.0, 0.0).astype(BF16)
    own = _own_head_lanes(tq)

    def body(state):
        it, _ = state
        js = [first_qb + i - it for i in range(nq)]
        starts = [pl.multiple_of(jnp.maximum(j, 0) * tq, tq) for j in js]
        zs = [_mm(qs[i], k_ref[pl.ds(starts[i], tq), :], NT, 3) for i in range(nq)]
        log_keeps = [-_softplus(z) for z in zs]
        masks = [col < row + jnp.where(j >= 0, it * tq, -tq) for j in js]
        css = [_mm_exact_rhs(jnp.where(m, lk, 0.0), cs_mat) for m, lk in zip(masks, log_keeps)]
        atts = []
        more = jnp.bool_(False)
        for i in range(nq):
            carry = carry_ref[i]
            atts.append(jnp.where(masks[i], jnp.exp(zs[i] + log_keeps[i] + css[i][:, 0:tq] + carry), 0.0))
            carry = carry + css[i][:, tq:2 * tq]
            carry_ref[i] = carry
            more = more | ((js[i] > 0) & (jnp.max(carry) > SB_LOG_UNDERFLOW))
        avs = [_mm(atts[i], v_ref[pl.ds(starts[i], tq), :]) for i in range(nq)]
        for i in range(nq):
            acc_ref[i] += _unstack_heads(jnp.where(own, avs[i], 0.0), tq)
        return it + 1, more.astype(jnp.int32)

    lax.while_loop(lambda s: s[1] > 0, body, (jnp.int32(0), jnp.int32(1)))
    for i in range(nq):
        o_ref[i * tq:(i + 1) * tq, :] = (acc_ref[i] * _silu(gate_ref[i * tq:(i + 1) * tq, :])).astype(o_ref.dtype)


def _sb(pd):
    b, t, _ = pd.shape
    tq = SB_BLOCK
    nq = SB_QBLOCKS
    blk = lambda cidx: pl.BlockSpec((None, nq * tq, GROUP), lambda i, j: (i, j, cidx))
    full = lambda cidx: pl.BlockSpec((None, t, GROUP), lambda i, j: (i, 0, cidx))
    return pl.pallas_call(
        _sb_kernel,
        grid=(b, t // (nq * tq)),
        in_specs=[blk(0), full(1), full(2), blk(3)],
        out_specs=pl.BlockSpec((None, nq * tq, GROUP), lambda i, j: (i, j, 0)),
        out_shape=jax.ShapeDtypeStruct((b, t, GROUP), BF16),
        scratch_shapes=[pltpu.VMEM((nq, tq, GROUP), F32), pltpu.VMEM((nq, N_HEADS * tq, tq), F32)],
        compiler_params=pltpu.CompilerParams(dimension_semantics=("arbitrary", "arbitrary"),
                                             vmem_limit_bytes=VMEM_LIMIT),
        name="stick_breaking",
    )(pd, pd, pd, pd)


def _memkv_kernel(m_ref, g_ref, wk_ref, wv_ref, k_ref, v_ref):
    h = (_rms_scale(m_ref[...]) * g_ref[...]).astype(BF16)
    k_ref[...] = _dg(h, wk_ref[...]).astype(k_ref.dtype)
    v_ref[...] = _dg(h, wv_ref[...]).astype(v_ref.dtype)


def _memkv(mem2d, g, wk, wv):
    n, d = mem2d.shape
    tm = ROW_TILE
    row = lambda i: (i, 0)
    fixed = lambda i: (0, 0)
    return pl.pallas_call(
        _memkv_kernel,
        grid=(n // tm,),
        in_specs=[pl.BlockSpec((tm, d), row), pl.BlockSpec((1, d), fixed),
                  pl.BlockSpec((d, d), fixed), pl.BlockSpec((d, d), fixed)],
        out_specs=[pl.BlockSpec((tm, d), row)] * 2,
        out_shape=[jax.ShapeDtypeStruct((n, d), BF16)] * 2,
        compiler_params=pltpu.CompilerParams(dimension_semantics=("arbitrary",), vmem_limit_bytes=VMEM_LIMIT),
        name="mem_kv",
    )(mem2d, g, wk, wv)


def _post_kernel(x_ref, ya_ref, yb_ref, yc_ref, yd_ref, wout_ref, gx_ref, wq_ref, k_ref, v_ref, wo_ref,
                 gf_ref, o_ref, *, final):
    d = x_ref.shape[-1]
    hd = d // XATTN_HEADS
    x1 = x_ref[...]
    for g, y_ref in enumerate((ya_ref, yb_ref, yc_ref, yd_ref)):
        x1 = x1 + _dg(y_ref[...], wout_ref[g * GROUP:(g + 1) * GROUP, :])
    hx = (_rms_scale(x1) * gx_ref[...]).astype(BF16)
    q = _dg(hx, wq_ref[...])
    outs = []
    for h in range(XATTN_HEADS):
        sl = slice(h * hd, (h + 1) * hd)
        s = _dg(q[:, sl].astype(BF16), k_ref[:, sl], NT) * (hd ** -0.5)
        s = s - jnp.max(s, axis=-1, keepdims=True)
        e = jnp.exp(s)
        p = e / jnp.sum(e, axis=-1, keepdims=True)
        outs.append(_dg(p.astype(BF16), v_ref[:, sl]).astype(BF16))
    x2 = x1 + _dg(jnp.concatenate(outs, axis=-1), wo_ref[...])
    if final:
        x2 = _rms_scale(x2) * gf_ref[...]
    o_ref[...] = x2


def _post(x, ys, wout, gx, wq, kmem, vmem, wo, gf, final):
    b, t, d = x.shape
    m = kmem.shape[1]
    tm = ROW_TILE
    fixed = lambda i, j: (0, 0)
    rowblk = lambda w: pl.BlockSpec((None, tm, w), lambda i, j: (i, j, 0))
    return pl.pallas_call(
        functools.partial(_post_kernel, final=final),
        grid=(b, t // tm),
        in_specs=[rowblk(d)] + [rowblk(GROUP)] * 4
        + [pl.BlockSpec((d, d), fixed), pl.BlockSpec((1, d), fixed), pl.BlockSpec((d, d), fixed),
           pl.BlockSpec((None, m, d), lambda i, j: (i, 0, 0)), pl.BlockSpec((None, m, d), lambda i, j: (i, 0, 0)),
           pl.BlockSpec((d, d), fixed), pl.BlockSpec((1, d), fixed)],
        out_specs=rowblk(d),
        out_shape=jax.ShapeDtypeStruct((b, t, d), F32),
        compiler_params=pltpu.CompilerParams(dimension_semantics=("arbitrary", "arbitrary"),
                                             vmem_limit_bytes=VMEM_LIMIT),
        name="outproj_xattn",
    )(x, *ys, wout, gx, wq, kmem, vmem, wo, gf)


def _rows8(*rows):
    rows = [r.reshape(1, -1).astype(F32) for r in rows]
    width = rows[0].shape[1]
    return jnp.concatenate(rows + [jnp.zeros((8 - len(rows), width), F32)], axis=0)


def kernel(x, mem, norm_mix, w_in, rwkv_mu, rwkv_w0, rwkv_w_up, rwkv_a0, rwkv_a_up, rwkv_k_k, rwkv_k_a,
           rwkv_r_k, rwkv_ln_g, rwkv_ln_b, conv_w, conv_b, conv_ln_g, conv_ln_b, hgrn_lb_logits, hgrn_norm_g,
           w_out, norm_xattn, norm_mem, xattn_wq, xattn_wk, xattn_wv, xattn_wo, norm_final):
    b, t, d = x.shape
    depth = w_in.shape[0]
    n_shift = 3 * GROUP + 2 * RWKV_LORA
    n_rwkv = n_shift + GROUP
    lb_soft = jax.nn.softmax(hgrn_lb_logits.astype(F32), axis=0)
    lower_bounds = jnp.cumsum(lb_soft, axis=0) - lb_soft[0]
    mem2d = mem.reshape(-1, d)
    pad_lora = RWKV_SHIFT_COLS - n_shift
    for l in range(depth):
        w = w_in[l]
        wa = jnp.concatenate([w[:, :n_shift], jnp.zeros((d, pad_lora), F32), w[:, n_shift:n_rwkv]], axis=1)
        wb = w[:, n_rwkv:n_rwkv + 3 * GROUP]
        wc = w[:, n_rwkv + 3 * GROUP:n_rwkv + 7 * GROUP]
        wd = w[:, n_rwkv + 7 * GROUP:]
        pa, pb, pc, pd = _inproj(x.reshape(-1, d), norm_mix[l].reshape(1, d), wa.astype(BF16), wb.astype(BF16),
                                 wc.astype(BF16), wd.astype(BF16))
        pa = pa.reshape(b, t, -1)
        pb = pb.reshape(b, t, -1)
        pc = pc.reshape(b, t, -1)
        pd = pd.reshape(b, t, -1)

        mu = jnp.concatenate([rwkv_mu[l], jnp.zeros((pad_lora,), F32)]).reshape(1, -1)
        rvec = _rows8(rwkv_w0[l], rwkv_a0[l], rwkv_k_k[l], rwkv_k_a[l], rwkv_r_k[l], rwkv_ln_g[l], rwkv_ln_b[l])
        lora_rows = LANES - 2 * RWKV_LORA
        wup = jnp.concatenate([rwkv_w_up[l], jnp.zeros((RWKV_LORA + lora_rows, GROUP), F32)], axis=0)
        aup = jnp.concatenate([jnp.zeros((RWKV_LORA, GROUP), F32), rwkv_a_up[l],
                               jnp.zeros((lora_rows, GROUP), F32)], axis=0)
        y_a = _rwkv(pa, mu, rvec, wup, aup)

        cw = jnp.concatenate([conv_w[l], jnp.zeros((32 - CONV_WIDTH, GROUP), F32)], axis=0)
        y_b = _conv(pb, cw, _rows8(conv_b[l], conv_ln_g[l], conv_ln_b[l]))
        y_c = _hgrn(pc, _rows8(lower_bounds[l], hgrn_norm_g[l]))
        y_d = _sb(pd)

        kmem, vmem = _memkv(mem2d, norm_mem[l].reshape(1, d), xattn_wk[l].astype(BF16), xattn_wv[l].astype(BF16))
        x = _post(x, (y_a, y_b, y_c, y_d), w_out[l].astype(BF16), norm_xattn[l].reshape(1, d),
                  xattn_wq[l].astype(BF16), kmem.reshape(b, -1, d), vmem.reshape(b, -1, d),
                  xattn_wo[l].astype(BF16), norm_final.reshape(1, d), final=(l == depth - 1))
    return x
```

```python
import functools

import jax
import jax.numpy as jnp
from jax import lax
from jax.experimental import pallas as pl
from jax.experimental.pallas import tpu as pltpu

F32 = jnp.float32
BF16 = jnp.bfloat16

GROUP = 256
N_HEADS = 4
HEAD_DIM = GROUP // N_HEADS
RWKV_LORA = 32
CONV_WIDTH = 31
XATTN_HEADS = 4
RMS_EPS = 1e-6
LN_EPS = 1e-5
RWKV_LN_EPS = 64e-5

LANES = 128
RWKV_SHIFT_COLS = 3 * GROUP + LANES
RWKV_COLS = RWKV_SHIFT_COLS + GROUP

CHUNK = 64
RWKV_TILE = 2 * CHUNK
SUB = 16
HGRN_TILE = 2 * CHUNK
HGRN_PASSES = 1
HGRN_DIAG_PIECES = 1
SB_BLOCK = 128
SB_QBLOCKS = 2
SB_CUMSUM_PIECES = 2
CONV_TILE = 512
CONV_HALO = 32
ROW_TILE = 256
POST_ROW_TILE = 512
SB_LOG_UNDERFLOW = -120.0
NEG_BIG = -1e30
VMEM_LIMIT = 56 * 1024 * 1024

NN = ((1,), (0,))
NT = ((1,), (1,))
TN = ((0,), (0,))


def _pieces(x, n):
    out = []
    for _ in range(n - 1):
        p = x.astype(BF16)
        out.append(p)
        x = x - p.astype(F32)
    out.append(x.astype(BF16))
    return out


def _dg(a, b, dims=NN):
    return lax.dot_general(a, b, (dims, ((), ())), preferred_element_type=F32)


def _mm(a, b, dims=NN, passes=1):
    if passes == 1:
        return _dg(a.astype(BF16), b.astype(BF16), dims)
    a_hi, a_lo = _pieces(a, 2)
    b_hi, b_lo = _pieces(b, 2)
    return _dg(jnp.concatenate([a_hi, a_hi, a_lo], axis=dims[0][0]),
               jnp.concatenate([b_hi, b_lo, b_hi], axis=dims[1][0]), dims)


def _mm_exact_rhs(x, w_bf16, n=3):
    return _dg(jnp.concatenate(_pieces(x, n), axis=1), jnp.concatenate([w_bf16] * n, axis=0))


def _mm_exact_lhs(w_bf16, x, n=3):
    return _dg(jnp.concatenate([w_bf16] * n, axis=1), jnp.concatenate(_pieces(x, n), axis=0))


def _iota2(shape, dim):
    return lax.broadcasted_iota(jnp.int32, shape, dim)


def _log2(n):
    assert n & (n - 1) == 0
    return n.bit_length() - 1


HEAD_SHIFT = _log2(HEAD_DIM)


def _same_head_ones(n=GROUP):
    r = _iota2((n, n), 0) >> HEAD_SHIFT
    c = _iota2((n, n), 1) >> HEAD_SHIFT
    return jnp.where(r == c, 1.0, 0.0).astype(BF16)


def _lower_ones(n):
    return jnp.where(_iota2((n, n), 1) <= _iota2((n, n), 0), 1.0, 0.0).astype(BF16)


def _stack_heads(x):
    head = _iota2(x.shape, 1) >> HEAD_SHIFT
    return jnp.concatenate([jnp.where(head == h, x, 0.0) for h in range(N_HEADS)], axis=0)


def _unstack_heads(y, t):
    return functools.reduce(lambda s, u: s + u, [y[h * t:(h + 1) * t] for h in range(N_HEADS)])


def _own_head_lanes(t):
    return (_iota2((N_HEADS * t, GROUP), 1) >> HEAD_SHIFT) == (_iota2((N_HEADS * t, GROUP), 0) >> _log2(t))


def _sigmoid(x):
    return 1.0 / (1.0 + jnp.exp(-x))


def _silu(x):
    return x * _sigmoid(x)


def _softplus(x):
    return jnp.maximum(x, 0.0) + jnp.log(1.0 + jnp.exp(-jnp.abs(x)))


def _rms_scale(x):
    return x * lax.rsqrt(jnp.mean(x * x, axis=-1, keepdims=True) + RMS_EPS)


def _inproj_kernel(x_ref, g_ref, wa_ref, wb_ref, wc_ref, wd_ref, pa_ref, pb_ref, pc_ref, pd_ref):
    hb = (_rms_scale(x_ref[...]) * g_ref[...]).astype(BF16)
    pa_ref[...] = _dg(hb, wa_ref[...])
    pb_ref[...] = _dg(hb, wb_ref[...])
    pc_ref[...] = _dg(hb, wc_ref[...])
    pd_ref[...] = _dg(hb, wd_ref[...])


def _layer(arr, l):
    shape = arr.shape[1:]
    return pl.BlockSpec((None,) + shape, lambda *_: (l,) + (0,) * len(shape))


def _inproj(x2d, g, wa, wb, wc, wd, l):
    n, d = x2d.shape
    tm = ROW_TILE
    row = lambda i: (i, 0)
    widths = (wa.shape[-1], wb.shape[-1], wc.shape[-1], wd.shape[-1])
    return pl.pallas_call(
        _inproj_kernel,
        grid=(n // tm,),
        in_specs=[pl.BlockSpec((tm, d), row)] + [_layer(a, l) for a in (g, wa, wb, wc, wd)],
        out_specs=[pl.BlockSpec((tm, w), row) for w in widths],
        out_shape=[jax.ShapeDtypeStruct((n, w), F32) for w in widths],
        compiler_params=pltpu.CompilerParams(dimension_semantics=("arbitrary",), vmem_limit_bytes=VMEM_LIMIT),
        name="inproj",
    )(x2d, g, wa, wb, wc, wd)


def _rwkv_rows(p, prev_row, mu, vec, wup, aup):
    row = _iota2(p.shape, 0)
    p_prev = jnp.where(row == 0, prev_row, pltpu.roll(p, 1, 0))
    m = p + (p_prev - p) * mu
    r = m[:, 0:GROUP]
    k = m[:, GROUP:2 * GROUP]
    v = m[:, 2 * GROUP:3 * GROUP]
    lora = m[:, 3 * GROUP:RWKV_SHIFT_COLS]
    w0, a0, k_k, k_a, r_k = vec[0:1, :], vec[1:2, :], vec[2:3, :], vec[3:4, :], vec[4:5, :]
    gseg = _same_head_ones()
    w_log = -_softplus(-(w0 + _mm(jnp.tanh(lora), wup, NN, 3))) - 0.5
    lw = -jnp.exp(w_log)
    a_sig = _sigmoid(a0 + _mm(lora, aup, NN, 3))
    kk = k * k_k
    kk = kk / jnp.maximum(jnp.sqrt(_mm_exact_rhs(kk * kk, gseg)), 1e-12)
    k2 = k * (1.0 + (a_sig - 1.0) * k_a)
    bonus = _mm_exact_rhs(r * k2 * r_k, gseg) * v
    return r, k2, v, kk, a_sig, lw, bonus


def _rwkv_prep(chains):
    c = CHUNK
    n = N_HEADS * c
    ltri = _lower_ones(c)
    ri = _iota2((n, n), 0)
    ci = _iota2((n, n), 1)
    cshift = _log2(c)
    same = (ri >> cshift) == (ci >> cshift)
    strict = same & (ci < ri)
    incl = same & (ci <= ri)
    fs = []
    for r, k2, v, kk, a_sig, lw in chains:
        cum = _mm_exact_lhs(ltri, lw)
        cum_last = cum[c - 1:c, :]
        g_inv = jnp.exp(-cum)
        g_rem = jnp.exp(cum_last - cum)
        fs.append(dict(a_t=_stack_heads(-kk * jnp.exp(cum - lw)),
                       r_t=_stack_heads(r * jnp.exp(cum)),
                       b_t=_stack_heads(kk * a_sig * g_inv), k_t=_stack_heads(k2 * g_inv),
                       b_h=_stack_heads(kk * a_sig * g_rem), k_h=_stack_heads(k2 * g_rem),
                       v_s=_stack_heads(v), decay=jnp.exp(cum_last)))
    pairs = [_dg(jnp.concatenate([f["a_t"], f["r_t"]], axis=0).astype(BF16),
                 jnp.concatenate([f["b_t"], f["k_t"]], axis=0).astype(BF16), NT) for f in fs]
    for f, m in zip(fs, pairs):
        f["a_ab"] = jnp.where(strict, m[0:n, 0:n], 0.0)
        f["a_ak"] = jnp.where(strict, m[0:n, n:2 * n], 0.0)
        f["a_rb"] = jnp.where(incl, m[n:2 * n, 0:n], 0.0)
        f["a_rk"] = jnp.where(incl, m[n:2 * n, n:2 * n], 0.0)

    eye = jnp.where(ri == ci, 1.0, 0.0)
    for f in fs:
        f["tinv"] = eye + jnp.where((ri >> 1) == (ci >> 1), f["a_ab"], 0.0)
    for s in range(2, cshift + 1):
        lower_left = ((ri >> s) == (ci >> s)) & ((ri >> (s - 1)) != (ci >> (s - 1)))
        xs = [_mm(jnp.where(lower_left, f["a_ab"], 0.0), f["tinv"]) for f in fs]
        xs = [_mm(f["tinv"], x) for f, x in zip(fs, xs)]
        for f, x in zip(fs, xs):
            f["tinv"] = f["tinv"] + x
    for f in fs:
        f["w_m"] = _mm(f["tinv"], f["a_t"])
    xs = [_mm(f["a_ak"], f["v_s"]) for f in fs]
    for f, x in zip(fs, xs):
        f["z_m"] = _mm(f["tinv"], x)
    for f in fs:
        f["a_r"] = jnp.concatenate([f["a_rb"], f["a_rk"]], axis=1).astype(BF16)
        f["bk_h"] = jnp.concatenate([f["b_h"], f["k_h"]], axis=0).astype(BF16)
        f["wr"] = jnp.concatenate([f["w_m"], f["r_t"]], axis=0).astype(BF16)
        f["v_s"] = f["v_s"].astype(BF16)
    return fs


def _rwkv_step(fs, states):
    n = N_HEADS * CHUNK
    sb = [s.astype(BF16) for s in states]
    wrs = [_dg(f["wr"], s, NT) for f, s in zip(fs, sb)]
    uvs = [jnp.concatenate([(wr[0:n] + f["z_m"]).astype(BF16), f["v_s"]], axis=0) for f, wr in zip(fs, wrs)]
    ys = [wr[n:2 * n] + _dg(f["a_r"], uv) for f, wr, uv in zip(fs, wrs, uvs)]
    new = [s * f["decay"] + _dg(uv, f["bk_h"], TN) for f, uv, s in zip(fs, uvs, states)]
    return ys, new


def _rwkv_kernel(p_ref, mu_ref, vec_ref, wup_ref, aup_ref, o_ref, s_ref, prev_ref):
    @pl.when(pl.program_id(0) == 0)
    def _():
        s_ref[...] = jnp.zeros_like(s_ref)
        prev_ref[...] = jnp.zeros_like(prev_ref)

    c = CHUNK
    nb, tile = p_ref.shape[0], p_ref.shape[1]
    vec = vec_ref[...]
    ln_g, ln_b = vec[5:6, :], vec[6:7, :]
    gseg = _same_head_ones()
    nch = tile // c
    rows = []
    for bi in range(nb):
        p = p_ref[bi, :, 0:RWKV_SHIFT_COLS]
        rows.append(_rwkv_rows(p, prev_ref[bi, 0:1, :], mu_ref[...], vec, wup_ref[...], aup_ref[...]))
        prev_ref[bi, 0:1, :] = p[tile - 1:tile, :]
    factors = _rwkv_prep([tuple(a[j * c:(j + 1) * c] for a in rows[bi][:6]) for j in range(nch) for bi in range(nb)])
    states = [s_ref[bi] for bi in range(nb)]
    y_parts = [[] for _ in range(nb)]
    for j in range(nch):
        ys, states = _rwkv_step(factors[j * nb:(j + 1) * nb], states)
        for bi in range(nb):
            y_parts[bi].append(_unstack_heads(ys[bi], c))
    for bi in range(nb):
        s_ref[bi] = states[bi]
        y = jnp.concatenate(y_parts[bi], axis=0)
        mean = _mm_exact_rhs(y, gseg) * (1.0 / HEAD_DIM)
        yc = y - mean
        var = _mm_exact_rhs(yc * yc, gseg) * (1.0 / HEAD_DIM)
        yn = yc * lax.rsqrt(var + RWKV_LN_EPS) * ln_g + ln_b
        gate = p_ref[bi, :, RWKV_SHIFT_COLS:RWKV_COLS]
        o_ref[bi] = ((yn + rows[bi][6]) * _silu(gate)).astype(o_ref.dtype)


def _rwkv(pa, mu, vec, wup, aup, l):
    b, t, _ = pa.shape
    return pl.pallas_call(
        _rwkv_kernel,
        grid=(t // RWKV_TILE,),
        in_specs=[pl.BlockSpec((b, RWKV_TILE, RWKV_COLS), lambda j: (0, j, 0))]
        + [_layer(a, l) for a in (mu, vec, wup, aup)],
        out_specs=pl.BlockSpec((b, RWKV_TILE, GROUP), lambda j: (0, j, 0)),
        out_shape=jax.ShapeDtypeStruct((b, t, GROUP), BF16),
        scratch_shapes=[pltpu.VMEM((b, GROUP, GROUP), F32), pltpu.VMEM((b, 8, RWKV_SHIFT_COLS), F32)],
        compiler_params=pltpu.CompilerParams(dimension_semantics=("arbitrary",), vmem_limit_bytes=VMEM_LIMIT),
        name="rwkv7",
    )(pa, mu, vec, wup, aup)


def _conv_kernel(val_ref, glu_ref, gate_ref, w_ref, vec_ref, o_ref, u_ref, sh_ref):
    tt = CONV_TILE
    sub = 8

    @pl.when(pl.program_id(1) == 0)
    def _():
        u_ref[0:CONV_HALO, :] = jnp.zeros((CONV_HALO, GROUP), F32)

    u_ref[CONV_HALO:CONV_HALO + tt, :] = val_ref[...] * _sigmoid(glu_ref[...])
    base = CONV_HALO - (CONV_WIDTH - 1)
    rows = sh_ref.shape[1]
    for b in range(1, sub):
        sh_ref[b - 1] = u_ref[b:b + rows, :]
    acc = jnp.zeros((tt, GROUP), F32)
    for j in range(CONV_WIDTH):
        a, b = divmod(base + j, sub)
        src = u_ref[a * sub:a * sub + tt, :] if b == 0 else sh_ref[b - 1, a * sub:a * sub + tt, :]
        acc = acc + src * w_ref[j:j + 1, :]
    u_ref[0:CONV_HALO, :] = u_ref[tt:tt + CONV_HALO, :]
    y = acc + vec_ref[0:1, :]
    mu = jnp.mean(y, axis=-1, keepdims=True)
    yc = y - mu
    var = jnp.mean(yc * yc, axis=-1, keepdims=True)
    yn = yc * lax.rsqrt(var + LN_EPS) * vec_ref[1:2, :] + vec_ref[2:3, :]
    o_ref[...] = (_silu(yn) * _silu(gate_ref[...])).astype(o_ref.dtype)


def _conv(pb, w, vec, l):
    b, t, _ = pb.shape
    tt = CONV_TILE
    col = lambda cidx: pl.BlockSpec((None, tt, GROUP), lambda i, j: (i, j, cidx))
    return pl.pallas_call(
        _conv_kernel,
        grid=(b, t // tt),
        in_specs=[col(0), col(1), col(2), _layer(w, l), _layer(vec, l)],
        out_specs=pl.BlockSpec((None, tt, GROUP), lambda i, j: (i, j, 0)),
        out_shape=jax.ShapeDtypeStruct((b, t, GROUP), BF16),
        scratch_shapes=[pltpu.VMEM((CONV_HALO + tt, GROUP), F32), pltpu.VMEM((7, CONV_HALO + tt - 8, GROUP), F32)],
        compiler_params=pltpu.CompilerParams(dimension_semantics=("arbitrary", "arbitrary"),
                                             vmem_limit_bytes=VMEM_LIMIT),
        name="conformer_conv",
    )(pb, pb, pb, w, vec)


def _hgrn_diag_rows(i, s):
    half = SUB // 2
    per_sub = SUB * half + half * half
    if s < half:
        return i * per_sub + s * SUB, SUB, 0
    return i * per_sub + half * SUB + (s - half) * half, half, half


def _hgrn_kernel(p_ref, vec_ref, o_ref, s_ref, b_ref, k_ref, v_ref, p_buf, r_buf):
    @pl.when(pl.program_id(0) == 0)
    def _():
        s_ref[...] = jnp.zeros_like(s_ref)

    c = CHUNK
    nb, tile = p_ref.shape[0], p_ref.shape[1]
    nch = tile // c
    nsub = c // SUB
    sshift = _log2(SUB)
    lb, norm_g = vec_ref[0:1, :], vec_ref[1:2, :]
    gseg = _same_head_ones()
    ltri = _lower_ones(c)
    chains = [(bi, j) for j in range(nch) for bi in range(nb)]
    q, kf, v, bcum = [], [], [], []
    for bi, j in chains:
        rows = slice(j * c, (j + 1) * c)
        q.append(_silu(p_ref[bi, rows, 0:GROUP]))
        f = lb + (1.0 - lb) * _sigmoid(p_ref[bi, rows, GROUP:2 * GROUP])
        kf.append(1.0 - f)
        v.append(p_ref[bi, rows, 2 * GROUP:3 * GROUP])
        bcum.append(jnp.log(f))
    bcum = [_mm_exact_lhs(ltri, lf) for lf in bcum]
    for n in range(len(chains)):
        b_ref[n] = bcum[n]
        k_ref[n] = kf[n]
        v_ref[n] = v[n]
    b_last = [b[c - 1:c, :] for b in bcum]
    same_head = (_iota2((GROUP, GROUP), 0) >> HEAD_SHIFT) == (_iota2((GROUP, GROUP), 1) >> HEAD_SHIFT)
    kv = [jnp.where(same_head, _mm(v[n], kf[n] * jnp.exp(b_last[n] - bcum[n]), TN, HGRN_PASSES), 0.0)
          for n in range(len(chains))]

    trow = _iota2((c, GROUP), 0)
    tsub = trow >> sshift
    q_off, k_cat = [], []
    for n in range(len(chains)):
        beta = b_ref[n, SUB - 1:SUB, :]
        for i in range(2, nsub):
            beta = jnp.where(tsub == i, b_ref[n, i * SUB - 1:i * SUB, :], beta)
        q_off.append(_stack_heads(q[n] * jnp.exp(jnp.where(tsub >= 1, bcum[n] - beta, NEG_BIG))))
        k_cat.append(jnp.concatenate(
            [kf[n] * jnp.exp(jnp.where(trow < i * SUB, b_ref[n, i * SUB - 1:i * SUB, :] - bcum[n], NEG_BIG))
             for i in range(1, nsub)], axis=0))
    att = [_mm(qo, kc, NT, HGRN_PASSES) for qo, kc in zip(q_off, k_cat)]
    arow = (_iota2(att[0].shape, 0) & (c - 1)) >> sshift
    acol = _iota2(att[0].shape, 1) >> _log2(c)
    att = [jnp.where(acol + 1 == arow, a, 0.0) for a in att]
    o_off = [_mm(a, jnp.concatenate([vn] * (nsub - 1), axis=0)) for a, vn in zip(att, v)]
    own = _own_head_lanes(c)
    o_off = [_unstack_heads(jnp.where(own, o, 0.0), c) for o in o_off]

    half = SUB // 2
    for n in range(len(chains)):
        for i in range(nsub):
            for s in range(SUB):
                src = i * SUB + s
                start, nrows, first = _hgrn_diag_rows(i, s)
                tgt = slice(i * SUB + first, (i + 1) * SUB)
                tau = _iota2((nrows, GROUP), 0) + first
                prod = (q[n][tgt] * jnp.exp(jnp.where(tau >= s, bcum[n][tgt] - b_ref[n, src:src + 1, :], NEG_BIG))
                        * k_ref[n, src:src + 1, :])
                p_buf[n, start:start + nrows, :] = prod
    for n in range(len(chains)):
        r_buf[n] = _mm_exact_rhs(p_buf[n], gseg, HGRN_DIAG_PIECES)
    o_diag = []
    for n in range(len(chains)):
        parts = []
        for i in range(nsub):
            lo = jnp.zeros((half, GROUP), F32)
            hi = jnp.zeros((half, GROUP), F32)
            for s in range(SUB):
                src = i * SUB + s
                start, nrows, first = _hgrn_diag_rows(i, s)
                vs = v_ref[n, src:src + 1, :]
                if first == 0:
                    lo = lo + r_buf[n, start:start + half, :] * vs
                    hi = hi + r_buf[n, start + half:start + SUB, :] * vs
                else:
                    hi = hi + r_buf[n, start:start + half, :] * vs
            parts += [lo, hi]
        o_diag.append(jnp.concatenate(parts, axis=0))

    states = [s_ref[bi] for bi in range(nb)]
    for j in range(nch):
        ns = range(j * nb, (j + 1) * nb)
        o_inter = [_mm(q[n] * jnp.exp(bcum[n]), states[n - j * nb], NT, HGRN_PASSES) for n in ns]
        states = [states[n - j * nb] * jnp.exp(b_last[n]) + kv[n] for n in ns]
        for n, oi in zip(ns, o_inter):
            bi = n - j * nb
            o = oi + o_off[n] + o_diag[n]
            ms = _mm_exact_rhs(o * o, gseg) * (1.0 / HEAD_DIM)
            o = o * lax.rsqrt(ms + RMS_EPS) * norm_g
            gate = p_ref[bi, j * c:(j + 1) * c, 3 * GROUP:4 * GROUP]
            o_ref[bi, j * c:(j + 1) * c, :] = (o * _silu(gate)).astype(o_ref.dtype)
    for bi in range(nb):
        s_ref[bi] = states[bi]


def _hgrn(pc, vec, l):
    b, t, _ = pc.shape
    nchains = b * (HGRN_TILE // CHUNK)
    diag_rows = (CHUNK // SUB) * (SUB * SUB // 2 + SUB * SUB // 4)
    return pl.pallas_call(
        _hgrn_kernel,
        grid=(t // HGRN_TILE,),
        in_specs=[pl.BlockSpec((b, HGRN_TILE, 4 * GROUP), lambda j: (0, j, 0)), _layer(vec, l)],
        out_specs=pl.BlockSpec((b, HGRN_TILE, GROUP), lambda j: (0, j, 0)),
        out_shape=jax.ShapeDtypeStruct((b, t, GROUP), BF16),
        scratch_shapes=[pltpu.VMEM((b, GROUP, GROUP), F32)] + [pltpu.VMEM((nchains, CHUNK, GROUP), F32)] * 3
        + [pltpu.VMEM((nchains, diag_rows, GROUP), F32)] * 2,
        compiler_params=pltpu.CompilerParams(dimension_semantics=("arbitrary",), vmem_limit_bytes=VMEM_LIMIT),
        name="hgrn2",
    )(pc, vec)


def _sb_kernel(q_ref, k_ref, v_ref, gate_ref, o_ref, acc_ref, carry_ref):
    tq = SB_BLOCK
    nq = SB_QBLOCKS
    n = N_HEADS * tq
    first_qb = pl.program_id(1) * nq
    qs = [_stack_heads(q_ref[i * tq:(i + 1) * tq, :] * (HEAD_DIM ** -0.5)) for i in range(nq)]
    acc_ref[...] = jnp.zeros_like(acc_ref)
    carry_ref[...] = jnp.zeros_like(carry_ref)

    row = _iota2((n, tq), 0) & (tq - 1)
    col = _iota2((n, tq), 1)
    cs_mat = jnp.where(_iota2((tq, tq), 0) > _iota2((tq, tq), 1), 1.0, 0.0).astype(BF16)
    own = _own_head_lanes(tq)

    def body(state):
        it, _ = state
        js = [first_qb + i - it for i in range(nq)]
        starts = [pl.multiple_of(jnp.maximum(j, 0) * tq, tq) for j in js]
        zs = [_mm(qs[i], k_ref[pl.ds(starts[i], tq), :], NT) for i in range(nq)]
        log_keeps = [-_softplus(z) for z in zs]
        masks = [col < row + jnp.where(j >= 0, it * tq, -tq) for j in js]
        lkms = [jnp.where(m, lk, 0.0) for m, lk in zip(masks, log_keeps)]
        css = [_mm_exact_rhs(lkm, cs_mat, SB_CUMSUM_PIECES) for lkm in lkms]
        atts = []
        more = jnp.bool_(False)
        for i in range(nq):
            carry = carry_ref[i]
            atts.append(jnp.where(masks[i], jnp.exp(zs[i] + log_keeps[i] + css[i] + carry), 0.0))
            carry = carry + jnp.sum(lkms[i], axis=1, keepdims=True)
            carry_ref[i] = carry
            more = more | ((js[i] > 0) & (jnp.max(carry) > SB_LOG_UNDERFLOW))
        avs = [_mm(atts[i], v_ref[pl.ds(starts[i], tq), :]) for i in range(nq)]
        for i in range(nq):
            acc_ref[i] += _unstack_heads(jnp.where(own, avs[i], 0.0), tq)
        return it + 1, more.astype(jnp.int32)

    lax.while_loop(lambda s: s[1] > 0, body, (jnp.int32(0), jnp.int32(1)))
    for i in range(nq):
        o_ref[i * tq:(i + 1) * tq, :] = (acc_ref[i] * _silu(gate_ref[i * tq:(i + 1) * tq, :])).astype(o_ref.dtype)


def _sb(pd):
    b, t, _ = pd.shape
    tq = SB_BLOCK
    nq = SB_QBLOCKS
    blk = lambda cidx: pl.BlockSpec((None, nq * tq, GROUP), lambda i, j: (i, j, cidx))
    full = lambda cidx: pl.BlockSpec((None, t, GROUP), lambda i, j: (i, 0, cidx))
    return pl.pallas_call(
        _sb_kernel,
        grid=(b, t // (nq * tq)),
        in_specs=[blk(0), full(1), full(2), blk(3)],
        out_specs=pl.BlockSpec((None, nq * tq, GROUP), lambda i, j: (i, j, 0)),
        out_shape=jax.ShapeDtypeStruct((b, t, GROUP), BF16),
        scratch_shapes=[pltpu.VMEM((nq, tq, GROUP), F32), pltpu.VMEM((nq, N_HEADS * tq, tq), F32)],
        compiler_params=pltpu.CompilerParams(dimension_semantics=("arbitrary", "arbitrary"),
                                             vmem_limit_bytes=VMEM_LIMIT),
        name="stick_breaking",
    )(pd, pd, pd, pd)


def _memkv_kernel(m_ref, g_ref, wk_ref, wv_ref, k_ref, v_ref):
    h = (_rms_scale(m_ref[...]) * g_ref[...]).astype(BF16)
    k_ref[...] = _dg(h, wk_ref[...]).astype(k_ref.dtype)
    v_ref[...] = _dg(h, wv_ref[...]).astype(v_ref.dtype)


def _memkv(mem2d, g, wk, wv, l):
    n, d = mem2d.shape
    tm = ROW_TILE
    row = lambda i: (i, 0)
    return pl.pallas_call(
        _memkv_kernel,
        grid=(n // tm,),
        in_specs=[pl.BlockSpec((tm, d), row), _layer(g, l), _layer(wk, l), _layer(wv, l)],
        out_specs=[pl.BlockSpec((tm, d), row)] * 2,
        out_shape=[jax.ShapeDtypeStruct((n, d), BF16)] * 2,
        compiler_params=pltpu.CompilerParams(dimension_semantics=("arbitrary",), vmem_limit_bytes=VMEM_LIMIT),
        name="mem_kv",
    )(mem2d, g, wk, wv)


def _post_kernel(x_ref, ya_ref, yb_ref, yc_ref, yd_ref, wout_ref, gx_ref, wq_ref, k_ref, v_ref, wo_ref,
                 gf_ref, o_ref, *, final):
    d = x_ref.shape[-1]
    hd = d // XATTN_HEADS
    x1 = x_ref[...]
    for g, y_ref in enumerate((ya_ref, yb_ref, yc_ref, yd_ref)):
        x1 = x1 + _dg(y_ref[...], wout_ref[g * GROUP:(g + 1) * GROUP, :])
    hx = (_rms_scale(x1) * gx_ref[...]).astype(BF16)
    q = _dg(hx, wq_ref[...])
    outs = []
    for h in range(XATTN_HEADS):
        sl = slice(h * hd, (h + 1) * hd)
        s = _dg(q[:, sl].astype(BF16), k_ref[:, sl], NT) * (hd ** -0.5)
        s = s - jnp.max(s, axis=-1, keepdims=True)
        e = jnp.exp(s)
        p = e / jnp.sum(e, axis=-1, keepdims=True)
        outs.append(_dg(p.astype(BF16), v_ref[:, sl]).astype(BF16))
    x2 = x1 + _dg(jnp.concatenate(outs, axis=-1), wo_ref[...])
    if final:
        x2 = _rms_scale(x2) * gf_ref[...]
    o_ref[...] = x2


def _post(x, ys, wout, gx, wq, kmem, vmem, wo, gf, l, final):
    b, t, d = x.shape
    m = kmem.shape[1]
    tm = POST_ROW_TILE
    rowblk = lambda w: pl.BlockSpec((None, tm, w), lambda i, j: (i, j, 0))
    return pl.pallas_call(
        functools.partial(_post_kernel, final=final),
        grid=(b, t // tm),
        in_specs=[rowblk(d)] + [rowblk(GROUP)] * 4
        + [_layer(wout, l), _layer(gx, l), _layer(wq, l),
           pl.BlockSpec((None, m, d), lambda i, j: (i, 0, 0)), pl.BlockSpec((None, m, d), lambda i, j: (i, 0, 0)),
           _layer(wo, l), pl.BlockSpec((1, d), lambda i, j: (0, 0))],
        out_specs=rowblk(d),
        out_shape=jax.ShapeDtypeStruct((b, t, d), F32),
        compiler_params=pltpu.CompilerParams(dimension_semantics=("arbitrary", "arbitrary"),
                                             vmem_limit_bytes=VMEM_LIMIT),
        name="outproj_xattn",
    )(x, *ys, wout, gx, wq, kmem, vmem, wo, gf)


def _rows8(*rows):
    rows = [r.reshape(r.shape[0], 1, -1).astype(F32) for r in rows]
    depth, _, width = rows[0].shape
    return jnp.concatenate(rows + [jnp.zeros((depth, 8 - len(rows), width), F32)], axis=1)


def kernel(x, mem, norm_mix, w_in, rwkv_mu, rwkv_w0, rwkv_w_up, rwkv_a0, rwkv_a_up, rwkv_k_k, rwkv_k_a,
           rwkv_r_k, rwkv_ln_g, rwkv_ln_b, conv_w, conv_b, conv_ln_g, conv_ln_b, hgrn_lb_logits, hgrn_norm_g,
           w_out, norm_xattn, norm_mem, xattn_wq, xattn_wk, xattn_wv, xattn_wo, norm_final):
    b, t, d = x.shape
    depth = w_in.shape[0]
    n_shift = 3 * GROUP + 2 * RWKV_LORA
    n_rwkv = n_shift + GROUP
    lb_soft = jax.nn.softmax(hgrn_lb_logits.astype(F32), axis=0)
    lower_bounds = jnp.cumsum(lb_soft, axis=0) - lb_soft[0]
    mem2d = mem.reshape(-1, d)
    pad_lora = RWKV_SHIFT_COLS - n_shift
    lora_rows = LANES - 2 * RWKV_LORA

    zeros = lambda *shape: jnp.zeros((depth,) + shape, F32)
    wa = jnp.concatenate([w_in[:, :, :n_shift], zeros(d, pad_lora), w_in[:, :, n_shift:n_rwkv]], axis=2).astype(BF16)
    wb = w_in[:, :, n_rwkv:n_rwkv + 3 * GROUP].astype(BF16)
    wc = w_in[:, :, n_rwkv + 3 * GROUP:n_rwkv + 7 * GROUP].astype(BF16)
    wd = w_in[:, :, n_rwkv + 7 * GROUP:].astype(BF16)
    g_mix = norm_mix.reshape(depth, 1, d)
    mu = jnp.concatenate([rwkv_mu, zeros(pad_lora)], axis=1).reshape(depth, 1, -1)
    rvec = _rows8(rwkv_w0, rwkv_a0, rwkv_k_k, rwkv_k_a, rwkv_r_k, rwkv_ln_g, rwkv_ln_b)
    wup = jnp.concatenate([rwkv_w_up, zeros(RWKV_LORA + lora_rows, GROUP)], axis=1)
    aup = jnp.concatenate([zeros(RWKV_LORA, GROUP), rwkv_a_up, zeros(lora_rows, GROUP)], axis=1)
    cw = jnp.concatenate([conv_w, zeros(32 - CONV_WIDTH, GROUP)], axis=1)
    cvec = _rows8(conv_b, conv_ln_g, conv_ln_b)
    hvec = _rows8(lower_bounds, hgrn_norm_g)
    g_mem = norm_mem.reshape(depth, 1, d)
    g_x = norm_xattn.reshape(depth, 1, d)
    wk, wv, wq, wo, wout = (a.astype(BF16) for a in (xattn_wk, xattn_wv, xattn_wq, xattn_wo, w_out))

    for l in range(depth):
        pa, pb, pc, pd = (p.reshape(b, t, -1) for p in _inproj(x.reshape(-1, d), g_mix, wa, wb, wc, wd, l))
        y_a = _rwkv(pa, mu, rvec, wup, aup, l)
        y_b = _conv(pb, cw, cvec, l)
        y_c = _hgrn(pc, hvec, l)
        y_d = _sb(pd)
        kmem, vmem = _memkv(mem2d, g_mem, wk, wv, l)
        x = _post(x, (y_a, y_b, y_c, y_d), wout, g_x, wq, kmem.reshape(b, -1, d), vmem.reshape(b, -1, d), wo,
                  norm_final.reshape(1, d), l, final=(l == depth - 1))
    return x
```

```python
import functools

import jax
import jax.numpy as jnp
from jax import lax
from jax.experimental import pallas as pl
from jax.experimental.pallas import tpu as pltpu

F32 = jnp.float32
BF16 = jnp.bfloat16

GROUP = 256
N_HEADS = 4
HEAD_DIM = GROUP // N_HEADS
RWKV_LORA = 32
CONV_WIDTH = 31
XATTN_HEADS = 4
RMS_EPS = 1e-6
LN_EPS = 1e-5
RWKV_LN_EPS = 64e-5

LANES = 128
RWKV_SHIFT_COLS = 3 * GROUP + LANES
RWKV_COLS = RWKV_SHIFT_COLS + GROUP

CHUNK = 64
RWKV_TILE = 2 * CHUNK
SUB = 16
HGRN_PASSES = 1
HGRN_DIAG_PIECES = 1
SB_BLOCK = 128
SB_QBLOCKS = 2
SB_CUMSUM_PIECES = 2
CONV_HALO = 32
ROW_TILE = 256
MIXIN_FILL_COLS = 256
POST_ROW_TILE = 512
SB_LOG_UNDERFLOW = -120.0
NEG_BIG = -1e30
VMEM_LIMIT = 56 * 1024 * 1024

NN = ((1,), (0,))
NT = ((1,), (1,))
TN = ((0,), (0,))


def _pieces(x, n):
    out = []
    for _ in range(n - 1):
        p = x.astype(BF16)
        out.append(p)
        x = x - p.astype(F32)
    out.append(x.astype(BF16))
    return out


def _dg(a, b, dims=NN):
    return lax.dot_general(a, b, (dims, ((), ())), preferred_element_type=F32)


def _mm(a, b, dims=NN, passes=1):
    if passes == 1:
        return _dg(a.astype(BF16), b.astype(BF16), dims)
    a_hi, a_lo = _pieces(a, 2)
    b_hi, b_lo = _pieces(b, 2)
    return _dg(jnp.concatenate([a_hi, a_hi, a_lo], axis=dims[0][0]),
               jnp.concatenate([b_hi, b_lo, b_hi], axis=dims[1][0]), dims)


def _mm_exact_rhs(x, w_bf16, n=3):
    return _dg(jnp.concatenate(_pieces(x, n), axis=1), jnp.concatenate([w_bf16] * n, axis=0))


def _mm_exact_lhs(w_bf16, x, n=3):
    return _dg(jnp.concatenate([w_bf16] * n, axis=1), jnp.concatenate(_pieces(x, n), axis=0))


def _iota2(shape, dim):
    return lax.broadcasted_iota(jnp.int32, shape, dim)


def _log2(n):
    assert n & (n - 1) == 0
    return n.bit_length() - 1


HEAD_SHIFT = _log2(HEAD_DIM)


def _same_head_ones(n=GROUP):
    r = _iota2((n, n), 0) >> HEAD_SHIFT
    c = _iota2((n, n), 1) >> HEAD_SHIFT
    return jnp.where(r == c, 1.0, 0.0).astype(BF16)


def _lower_ones(n):
    return jnp.where(_iota2((n, n), 1) <= _iota2((n, n), 0), 1.0, 0.0).astype(BF16)


def _stack_heads(x):
    head = _iota2(x.shape, 1) >> HEAD_SHIFT
    return jnp.concatenate([jnp.where(head == h, x, 0.0) for h in range(N_HEADS)], axis=0)


def _unstack_heads(y, t):
    return functools.reduce(lambda s, u: s + u, [y[h * t:(h + 1) * t] for h in range(N_HEADS)])


def _own_head_lanes(t):
    return (_iota2((N_HEADS * t, GROUP), 1) >> HEAD_SHIFT) == (_iota2((N_HEADS * t, GROUP), 0) >> _log2(t))


def _sigmoid(x):
    return 1.0 / (1.0 + jnp.exp(-x))


def _silu(x):
    return x * _sigmoid(x)


def _softplus(x):
    return jnp.maximum(x, 0.0) + jnp.log(1.0 + jnp.exp(-jnp.abs(x)))


def _rms_scale(x):
    return x * lax.rsqrt(jnp.mean(x * x, axis=-1, keepdims=True) + RMS_EPS)


def _layer(arr, l):
    shape = arr.shape[1:]
    return pl.BlockSpec((None,) + shape, lambda *_: (l,) + (0,) * len(shape))


def _rwkv_rows(p, prev_row, mu, vec, wup, aup):
    row = _iota2(p.shape, 0)
    p_prev = jnp.where(row == 0, prev_row, pltpu.roll(p, 1, 0))
    m = p + (p_prev - p) * mu
    r = m[:, 0:GROUP]
    k = m[:, GROUP:2 * GROUP]
    v = m[:, 2 * GROUP:3 * GROUP]
    lora = m[:, 3 * GROUP:RWKV_SHIFT_COLS]
    w0, a0, k_k, k_a, r_k = vec[0:1, :], vec[1:2, :], vec[2:3, :], vec[3:4, :], vec[4:5, :]
    gseg = _same_head_ones()
    w_log = -_softplus(-(w0 + _mm(jnp.tanh(lora), wup, NN, 3))) - 0.5
    lw = -jnp.exp(w_log)
    a_sig = _sigmoid(a0 + _mm(lora, aup, NN, 3))
    kk = k * k_k
    kk = kk / jnp.maximum(jnp.sqrt(_mm_exact_rhs(kk * kk, gseg)), 1e-12)
    k2 = k * (1.0 + (a_sig - 1.0) * k_a)
    bonus = _mm_exact_rhs(r * k2 * r_k, gseg) * v
    return r, k2, v, kk, a_sig, lw, bonus


def _rwkv_prep(chains):
    c = CHUNK
    n = N_HEADS * c
    ltri = _lower_ones(c)
    ri = _iota2((n, n), 0)
    ci = _iota2((n, n), 1)
    cshift = _log2(c)
    same = (ri >> cshift) == (ci >> cshift)
    strict = same & (ci < ri)
    incl = same & (ci <= ri)
    fs = []
    for r, k2, v, kk, a_sig, lw in chains:
        cum = _mm_exact_lhs(ltri, lw)
        cum_last = cum[c - 1:c, :]
        g_inv = jnp.exp(-cum)
        g_rem = jnp.exp(cum_last - cum)
        fs.append(dict(a_t=_stack_heads(-kk * jnp.exp(cum - lw)),
                       r_t=_stack_heads(r * jnp.exp(cum)),
                       b_t=_stack_heads(kk * a_sig * g_inv), k_t=_stack_heads(k2 * g_inv),
                       b_h=_stack_heads(kk * a_sig * g_rem), k_h=_stack_heads(k2 * g_rem),
                       v_s=_stack_heads(v), decay=jnp.exp(cum_last)))
    pairs = [_dg(jnp.concatenate([f["a_t"], f["r_t"]], axis=0).astype(BF16),
                 jnp.concatenate([f["b_t"], f["k_t"]], axis=0).astype(BF16), NT) for f in fs]
    for f, m in zip(fs, pairs):
        f["a_ab"] = jnp.where(strict, m[0:n, 0:n], 0.0)
        f["a_ak"] = jnp.where(strict, m[0:n, n:2 * n], 0.0)
        f["a_rb"] = jnp.where(incl, m[n:2 * n, 0:n], 0.0)
        f["a_rk"] = jnp.where(incl, m[n:2 * n, n:2 * n], 0.0)

    eye = jnp.where(ri == ci, 1.0, 0.0)
    for f in fs:
        f["tinv"] = eye + jnp.where((ri >> 1) == (ci >> 1), f["a_ab"], 0.0)
    for s in range(2, cshift + 1):
        lower_left = ((ri >> s) == (ci >> s)) & ((ri >> (s - 1)) != (ci >> (s - 1)))
        xs = [_mm(jnp.where(lower_left, f["a_ab"], 0.0), f["tinv"]) for f in fs]
        xs = [_mm(f["tinv"], x) for f, x in zip(fs, xs)]
        for f, x in zip(fs, xs):
            f["tinv"] = f["tinv"] + x
    for f in fs:
        f["w_m"] = _mm(f["tinv"], f["a_t"])
    xs = [_mm(f["a_ak"], f["v_s"]) for f in fs]
    for f, x in zip(fs, xs):
        f["z_m"] = _mm(f["tinv"], x)
    for f in fs:
        f["a_r"] = jnp.concatenate([f["a_rb"], f["a_rk"]], axis=1).astype(BF16)
        f["bk_h"] = jnp.concatenate([f["b_h"], f["k_h"]], axis=0).astype(BF16)
        f["wr"] = jnp.concatenate([f["w_m"], f["r_t"]], axis=0).astype(BF16)
        f["v_s"] = f["v_s"].astype(BF16)
    return fs


def _rwkv_step(fs, states):
    n = N_HEADS * CHUNK
    sb = [s.astype(BF16) for s in states]
    wrs = [_dg(f["wr"], s, NT) for f, s in zip(fs, sb)]
    uvs = [jnp.concatenate([(wr[0:n] + f["z_m"]).astype(BF16), f["v_s"]], axis=0) for f, wr in zip(fs, wrs)]
    ys = [wr[n:2 * n] + _dg(f["a_r"], uv) for f, wr, uv in zip(fs, wrs, uvs)]
    new = [s * f["decay"] + _dg(uv, f["bk_h"], TN) for f, uv, s in zip(fs, uvs, states)]
    return ys, new


def _rwkv_kernel(p_ref, mu_ref, vec_ref, wup_ref, aup_ref, o_ref, s_ref, prev_ref):
    @pl.when(pl.program_id(0) == 0)
    def _():
        s_ref[...] = jnp.zeros_like(s_ref)
        prev_ref[...] = jnp.zeros_like(prev_ref)

    c = CHUNK
    nb, tile = p_ref.shape[0], p_ref.shape[1]
    vec = vec_ref[...]
    ln_g, ln_b = vec[5:6, :], vec[6:7, :]
    gseg = _same_head_ones()
    nch = tile // c
    rows = []
    for bi in range(nb):
        p = p_ref[bi, :, 0:RWKV_SHIFT_COLS]
        rows.append(_rwkv_rows(p, prev_ref[bi, 0:1, :], mu_ref[...], vec, wup_ref[...], aup_ref[...]))
        prev_ref[bi, 0:1, :] = p[tile - 1:tile, :]
    factors = _rwkv_prep([tuple(a[j * c:(j + 1) * c] for a in rows[bi][:6]) for j in range(nch) for bi in range(nb)])
    states = [s_ref[bi] for bi in range(nb)]
    y_parts = [[] for _ in range(nb)]
    for j in range(nch):
        ys, states = _rwkv_step(factors[j * nb:(j + 1) * nb], states)
        for bi in range(nb):
            y_parts[bi].append(_unstack_heads(ys[bi], c))
    for bi in range(nb):
        s_ref[bi] = states[bi]
        y = jnp.concatenate(y_parts[bi], axis=0)
        mean = _mm_exact_rhs(y, gseg) * (1.0 / HEAD_DIM)
        yc = y - mean
        var = _mm_exact_rhs(yc * yc, gseg) * (1.0 / HEAD_DIM)
        yn = yc * lax.rsqrt(var + RWKV_LN_EPS) * ln_g + ln_b
        gate = p_ref[bi, :, RWKV_SHIFT_COLS:RWKV_COLS]
        o_ref[bi] = ((yn + rows[bi][6]) * _silu(gate)).astype(o_ref.dtype)


def _rwkv(pa, mu, vec, wup, aup, l):
    b, t, _ = pa.shape
    return pl.pallas_call(
        _rwkv_kernel,
        grid=(t // RWKV_TILE,),
        in_specs=[pl.BlockSpec((b, RWKV_TILE, RWKV_COLS), lambda j: (0, j, 0))]
        + [_layer(a, l) for a in (mu, vec, wup, aup)],
        out_specs=pl.BlockSpec((b, RWKV_TILE, GROUP), lambda j: (0, j, 0)),
        out_shape=jax.ShapeDtypeStruct((b, t, GROUP), BF16),
        scratch_shapes=[pltpu.VMEM((b, GROUP, GROUP), F32), pltpu.VMEM((b, 8, RWKV_SHIFT_COLS), F32)],
        compiler_params=pltpu.CompilerParams(dimension_semantics=("arbitrary",), vmem_limit_bytes=VMEM_LIMIT),
        name="rwkv7",
    )(pa, mu, vec, wup, aup)


def _conv_body(p_ref, w_ref, vec_ref, o_ref, u_ref, sh_ref):
    tt = p_ref.shape[0]
    sub = 8
    u_ref[CONV_HALO:CONV_HALO + tt, :] = p_ref[:, 0:GROUP] * _sigmoid(p_ref[:, GROUP:2 * GROUP])
    base = CONV_HALO - (CONV_WIDTH - 1)
    rows = sh_ref.shape[1]
    for b in range(1, sub):
        sh_ref[b - 1] = u_ref[b:b + rows, :]
    acc = jnp.zeros((tt, GROUP), F32)
    for j in range(CONV_WIDTH):
        a, b = divmod(base + j, sub)
        src = u_ref[a * sub:a * sub + tt, :] if b == 0 else sh_ref[b - 1, a * sub:a * sub + tt, :]
        acc = acc + src * w_ref[j:j + 1, :]
    u_ref[0:CONV_HALO, :] = u_ref[tt:tt + CONV_HALO, :]
    y = acc + vec_ref[0:1, :]
    mu = jnp.mean(y, axis=-1, keepdims=True)
    yc = y - mu
    var = jnp.mean(yc * yc, axis=-1, keepdims=True)
    yn = yc * lax.rsqrt(var + LN_EPS) * vec_ref[1:2, :] + vec_ref[2:3, :]
    o_ref[...] = (_silu(yn) * _silu(p_ref[:, 2 * GROUP:3 * GROUP])).astype(o_ref.dtype)


def _hgrn_diag_rows(i, s):
    half = SUB // 2
    per_sub = SUB * half + half * half
    if s < half:
        return i * per_sub + s * SUB, SUB, 0
    return i * per_sub + half * SUB + (s - half) * half, half, half


def _hgrn_body(p_ref, vec_ref, o_ref, s_ref, b_ref, k_ref, v_ref, p_buf, r_buf, fillers=()):
    fillers = list(fillers)

    def fill():
        if fillers:
            fillers.pop(0)()

    c = CHUNK
    nb, tile = p_ref.shape[0], p_ref.shape[1]
    nch = tile // c
    nsub = c // SUB
    sshift = _log2(SUB)
    lb, norm_g = vec_ref[0:1, :], vec_ref[1:2, :]
    gseg = _same_head_ones()
    ltri = _lower_ones(c)
    chains = [(bi, j) for j in range(nch) for bi in range(nb)]
    q, kf, v, bcum = [], [], [], []
    for bi, j in chains:
        rows = slice(j * c, (j + 1) * c)
        q.append(_silu(p_ref[bi, rows, 0:GROUP]))
        f = lb + (1.0 - lb) * _sigmoid(p_ref[bi, rows, GROUP:2 * GROUP])
        kf.append(1.0 - f)
        v.append(p_ref[bi, rows, 2 * GROUP:3 * GROUP])
        bcum.append(jnp.log(f))
    bcum = [_mm_exact_lhs(ltri, lf) for lf in bcum]
    for n in range(len(chains)):
        b_ref[n] = bcum[n]
        k_ref[n] = kf[n]
        v_ref[n] = v[n]
    b_last = [b[c - 1:c, :] for b in bcum]
    same_head = (_iota2((GROUP, GROUP), 0) >> HEAD_SHIFT) == (_iota2((GROUP, GROUP), 1) >> HEAD_SHIFT)
    kv = [jnp.where(same_head, _mm(v[n], kf[n] * jnp.exp(b_last[n] - bcum[n]), TN, HGRN_PASSES), 0.0)
          for n in range(len(chains))]

    trow = _iota2((c, GROUP), 0)
    tsub = trow >> sshift
    q_off, k_cat = [], []
    for n in range(len(chains)):
        beta = b_ref[n, SUB - 1:SUB, :]
        for i in range(2, nsub):
            beta = jnp.where(tsub == i, b_ref[n, i * SUB - 1:i * SUB, :], beta)
        q_off.append(_stack_heads(q[n] * jnp.exp(jnp.where(tsub >= 1, bcum[n] - beta, NEG_BIG))))
        k_cat.append(jnp.concatenate(
            [kf[n] * jnp.exp(jnp.where(trow < i * SUB, b_ref[n, i * SUB - 1:i * SUB, :] - bcum[n], NEG_BIG))
             for i in range(1, nsub)], axis=0))
    att = [_mm(qo, kc, NT, HGRN_PASSES) for qo, kc in zip(q_off, k_cat)]
    arow = (_iota2(att[0].shape, 0) & (c - 1)) >> sshift
    acol = _iota2(att[0].shape, 1) >> _log2(c)
    att = [jnp.where(acol + 1 == arow, a, 0.0) for a in att]
    o_off = [_mm(a, jnp.concatenate([vn] * (nsub - 1), axis=0)) for a, vn in zip(att, v)]
    own = _own_head_lanes(c)
    o_off = [_unstack_heads(jnp.where(own, o, 0.0), c) for o in o_off]

    half = SUB // 2
    for n in range(len(chains)):
        for i in range(nsub):
            for s in range(SUB):
                src = i * SUB + s
                start, nrows, first = _hgrn_diag_rows(i, s)
                tgt = slice(i * SUB + first, (i + 1) * SUB)
                tau = _iota2((nrows, GROUP), 0) + first
                prod = (q[n][tgt] * jnp.exp(jnp.where(tau >= s, bcum[n][tgt] - b_ref[n, src:src + 1, :], NEG_BIG))
                        * k_ref[n, src:src + 1, :])
                p_buf[n, start:start + nrows, :] = prod
        fill()
    for n in range(len(chains)):
        r_buf[n] = _mm_exact_rhs(p_buf[n], gseg, HGRN_DIAG_PIECES)
    o_diag = []
    for n in range(len(chains)):
        fill()
        parts = []
        for i in range(nsub):
            lo = jnp.zeros((half, GROUP), F32)
            hi = jnp.zeros((half, GROUP), F32)
            for s in range(SUB):
                src = i * SUB + s
                start, nrows, first = _hgrn_diag_rows(i, s)
                vs = v_ref[n, src:src + 1, :]
                if first == 0:
                    lo = lo + r_buf[n, start:start + half, :] * vs
                    hi = hi + r_buf[n, start + half:start + SUB, :] * vs
                else:
                    hi = hi + r_buf[n, start:start + half, :] * vs
            parts += [lo, hi]
        o_diag.append(jnp.concatenate(parts, axis=0))

    states = [s_ref[bi] for bi in range(nb)]
    for j in range(nch):
        ns = range(j * nb, (j + 1) * nb)
        o_inter = [_mm(q[n] * jnp.exp(bcum[n]), states[n - j * nb], NT, HGRN_PASSES) for n in ns]
        states = [states[n - j * nb] * jnp.exp(b_last[n]) + kv[n] for n in ns]
        for n, oi in zip(ns, o_inter):
            bi = n - j * nb
            o = oi + o_off[n] + o_diag[n]
            ms = _mm_exact_rhs(o * o, gseg) * (1.0 / HEAD_DIM)
            o = o * lax.rsqrt(ms + RMS_EPS) * norm_g
            gate = p_ref[bi, j * c:(j + 1) * c, 3 * GROUP:4 * GROUP]
            o_ref[bi, j * c:(j + 1) * c, :] = (o * _silu(gate)).astype(o_ref.dtype)
    for bi in range(nb):
        s_ref[bi] = states[bi]
    while fillers:
        fill()


def _mixin_kernel(x_ref, g_ref, wa_ref, wb_ref, wc_ref, wqg_ref, wkv_ref, cw_ref, cvec_ref, hvec_ref,
                  pa_ref, pqg_ref, pkv_ref, yb_ref, yc_ref,
                  pb_s, pc_s, u_ref, sh_ref, s_ref, b_ref, k_ref, v_ref, p_buf, r_buf):
    @pl.when(pl.program_id(1) == 0)
    def _():
        u_ref[0:CONV_HALO, :] = jnp.zeros((CONV_HALO, GROUP), F32)
        s_ref[...] = jnp.zeros_like(s_ref)

    hb = (_rms_scale(x_ref[...]) * g_ref[...]).astype(BF16)
    pb_s[...] = _dg(hb, wb_ref[...])
    pc_s[0] = _dg(hb, wc_ref[...])

    def project(w_ref, o_ref, c0, c1):
        def run():
            o_ref[:, c0:c1] = _dg(hb, w_ref[:, c0:c1]).astype(o_ref.dtype)
        return run

    pieces = [project(w_ref, o_ref, c0, min(c0 + MIXIN_FILL_COLS, w_ref.shape[1]))
              for w_ref, o_ref in ((wa_ref, pa_ref), (wqg_ref, pqg_ref), (wkv_ref, pkv_ref))
              for c0 in range(0, w_ref.shape[1], MIXIN_FILL_COLS)]
    pieces[0]()
    _conv_body(pb_s, cw_ref, cvec_ref, yb_ref, u_ref, sh_ref)
    _hgrn_body(pc_s, hvec_ref, yc_ref, s_ref, b_ref, k_ref, v_ref, p_buf, r_buf, fillers=pieces[1:])


def _mixin(x, g, wa, wb, wc, wqg, wkv, cw, cvec, hvec, l):
    b, t, d = x.shape
    tm = ROW_TILE
    nchains = tm // CHUNK
    diag_rows = (CHUNK // SUB) * (SUB * SUB // 2 + SUB * SUB // 4)
    rowblk = lambda w: pl.BlockSpec((None, tm, w), lambda i, j: (i, j, 0))
    widths = (wa.shape[-1], wqg.shape[-1], wkv.shape[-1])
    return pl.pallas_call(
        _mixin_kernel,
        grid=(b, t // tm),
        in_specs=[rowblk(d)] + [_layer(a, l) for a in (g, wa, wb, wc, wqg, wkv, cw, cvec, hvec)],
        out_specs=[rowblk(w) for w in widths] + [rowblk(GROUP), pl.BlockSpec((1, tm, GROUP), lambda i, j: (i, j, 0))],
        out_shape=[jax.ShapeDtypeStruct((b, t, w), dt) for w, dt in zip(widths, (F32, F32, BF16))]
        + [jax.ShapeDtypeStruct((b, t, GROUP), BF16)] * 2,
        scratch_shapes=[pltpu.VMEM((tm, wb.shape[-1]), F32), pltpu.VMEM((1, tm, wc.shape[-1]), F32),
                        pltpu.VMEM((CONV_HALO + tm, GROUP), F32), pltpu.VMEM((7, CONV_HALO + tm - 8, GROUP), F32),
                        pltpu.VMEM((1, GROUP, GROUP), F32)] + [pltpu.VMEM((nchains, CHUNK, GROUP), F32)] * 3
        + [pltpu.VMEM((nchains, diag_rows, GROUP), F32)] * 2,
        compiler_params=pltpu.CompilerParams(dimension_semantics=("arbitrary", "arbitrary"),
                                             vmem_limit_bytes=VMEM_LIMIT),
        name="inproj_conv_hgrn",
    )(x, g, wa, wb, wc, wqg, wkv, cw, cvec, hvec)


def _sb_kernel(q_ref, k_ref, v_ref, gate_ref, o_ref, acc_ref, carry_ref, vst_ref):
    tq = SB_BLOCK
    nq = SB_QBLOCKS
    n = N_HEADS * tq
    first_qb = pl.program_id(1) * nq

    @pl.when(pl.program_id(1) == 0)
    def _():
        def fill(j, c):
            vst_ref[j] = _stack_heads(v_ref[pl.ds(pl.multiple_of(j * tq, tq), tq), :])
            return c
        lax.fori_loop(0, v_ref.shape[0] // tq, fill, 0)

    qs = [_stack_heads(q_ref[i * tq:(i + 1) * tq, :] * (HEAD_DIM ** -0.5)).astype(BF16) for i in range(nq)]
    acc_ref[...] = jnp.zeros_like(acc_ref)
    carry_ref[...] = jnp.zeros_like(carry_ref)

    row = _iota2((n, tq), 0) & (tq - 1)
    col = _iota2((n, tq), 1)
    cs_mat = jnp.where(_iota2((tq, tq), 0) > _iota2((tq, tq), 1), 1.0, 0.0).astype(BF16)

    def body(state):
        it, _ = state
        js = [first_qb + i - it for i in range(nq)]
        jcs = [jnp.maximum(j, 0) for j in js]
        zs = [_dg(qs[i], k_ref[pl.ds(pl.multiple_of(jcs[i] * tq, tq), tq), :], NT) for i in range(nq)]
        log_keeps = [-_softplus(z) for z in zs]
        masks = [col < row + jnp.where(j >= 0, it * tq, -tq) for j in js]
        lkms = [jnp.where(m, lk, 0.0) for m, lk in zip(masks, log_keeps)]
        css = [_mm_exact_rhs(lkm, cs_mat, SB_CUMSUM_PIECES) for lkm in lkms]
        atts = []
        more = jnp.bool_(False)
        for i in range(nq):
            carry = carry_ref[i]
            att = jnp.where(masks[i], jnp.exp(zs[i] + log_keeps[i] + css[i] + carry), 0.0).astype(BF16)
            atts.append(jnp.concatenate([att[h * tq:(h + 1) * tq] for h in range(N_HEADS)], axis=1))
            carry = carry + jnp.sum(lkms[i], axis=1, keepdims=True)
            carry_ref[i] = carry
            more = more | ((js[i] > 0) & (jnp.max(carry) > SB_LOG_UNDERFLOW))
        avs = [_dg(atts[i], vst_ref[jcs[i]]) for i in range(nq)]
        for i in range(nq):
            acc_ref[i] += avs[i]
        return it + 1, more.astype(jnp.int32)

    lax.while_loop(lambda s: s[1] > 0, body, (jnp.int32(0), jnp.int32(1)))
    for i in range(nq):
        o_ref[i * tq:(i + 1) * tq, :] = (acc_ref[i] * _silu(gate_ref[i * tq:(i + 1) * tq, :])).astype(o_ref.dtype)


def _sb(pqg, pkv):
    b, t, _ = pqg.shape
    tq = SB_BLOCK
    nq = SB_QBLOCKS
    blk = lambda cidx: pl.BlockSpec((None, nq * tq, GROUP), lambda i, j: (i, j, cidx))
    full = lambda cidx: pl.BlockSpec((None, t, GROUP), lambda i, j: (i, 0, cidx))
    return pl.pallas_call(
        _sb_kernel,
        grid=(b, t // (nq * tq)),
        in_specs=[blk(0), full(0), full(1), blk(1)],
        out_specs=pl.BlockSpec((None, nq * tq, GROUP), lambda i, j: (i, j, 0)),
        out_shape=jax.ShapeDtypeStruct((b, t, GROUP), BF16),
        scratch_shapes=[pltpu.VMEM((nq, tq, GROUP), F32), pltpu.VMEM((nq, N_HEADS * tq, tq), F32),
                        pltpu.VMEM((t // tq, N_HEADS * tq, GROUP), BF16)],
        compiler_params=pltpu.CompilerParams(dimension_semantics=("arbitrary", "arbitrary"),
                                             vmem_limit_bytes=VMEM_LIMIT),
        name="stick_breaking",
    )(pqg, pkv, pkv, pqg)


def _memkv_kernel(m_ref, g_ref, wk_ref, wv_ref, k_ref, v_ref):
    h = (_rms_scale(m_ref[...]) * g_ref[...]).astype(BF16)
    k_ref[...] = _dg(h, wk_ref[...]).astype(k_ref.dtype)
    v_ref[...] = _dg(h, wv_ref[...]).astype(v_ref.dtype)


def _memkv(mem2d, g, wk, wv, l):
    n, d = mem2d.shape
    tm = ROW_TILE
    row = lambda i: (i, 0)
    return pl.pallas_call(
        _memkv_kernel,
        grid=(n // tm,),
        in_specs=[pl.BlockSpec((tm, d), row), _layer(g, l), _layer(wk, l), _layer(wv, l)],
        out_specs=[pl.BlockSpec((tm, d), row)] * 2,
        out_shape=[jax.ShapeDtypeStruct((n, d), BF16)] * 2,
        compiler_params=pltpu.CompilerParams(dimension_semantics=("arbitrary",), vmem_limit_bytes=VMEM_LIMIT),
        name="mem_kv",
    )(mem2d, g, wk, wv)


def _post_kernel(x_ref, ya_ref, yb_ref, yc_ref, yd_ref, wout_ref, gx_ref, wq_ref, k_ref, v_ref, wo_ref,
                 gf_ref, o_ref, *, final):
    d = x_ref.shape[-1]
    hd = d // XATTN_HEADS
    x1 = x_ref[...]
    for g, y_ref in enumerate((ya_ref, yb_ref, yc_ref, yd_ref)):
        x1 = x1 + _dg(y_ref[...], wout_ref[g * GROUP:(g + 1) * GROUP, :])
    hx = (_rms_scale(x1) * gx_ref[...]).astype(BF16)
    q = _dg(hx, wq_ref[...])
    outs = []
    for h in range(XATTN_HEADS):
        sl = slice(h * hd, (h + 1) * hd)
        s = _dg(q[:, sl].astype(BF16), k_ref[:, sl], NT) * (hd ** -0.5)
        s = s - jnp.max(s, axis=-1, keepdims=True)
        e = jnp.exp(s)
        p = e / jnp.sum(e, axis=-1, keepdims=True)
        outs.append(_dg(p.astype(BF16), v_ref[:, sl]).astype(BF16))
    x2 = x1 + _dg(jnp.concatenate(outs, axis=-1), wo_ref[...])
    if final:
        x2 = _rms_scale(x2) * gf_ref[...]
    o_ref[...] = x2


def _post(x, ys, wout, gx, wq, kmem, vmem, wo, gf, l, final):
    b, t, d = x.shape
    m = kmem.shape[1]
    tm = POST_ROW_TILE
    rowblk = lambda w: pl.BlockSpec((None, tm, w), lambda i, j: (i, j, 0))
    return pl.pallas_call(
        functools.partial(_post_kernel, final=final),
        grid=(b, t // tm),
        in_specs=[rowblk(d)] + [rowblk(GROUP)] * 4
        + [_layer(wout, l), _layer(gx, l), _layer(wq, l),
           pl.BlockSpec((None, m, d), lambda i, j: (i, 0, 0)), pl.BlockSpec((None, m, d), lambda i, j: (i, 0, 0)),
           _layer(wo, l), pl.BlockSpec((1, d), lambda i, j: (0, 0))],
        out_specs=rowblk(d),
        out_shape=jax.ShapeDtypeStruct((b, t, d), F32),
        compiler_params=pltpu.CompilerParams(dimension_semantics=("arbitrary", "arbitrary"),
                                             vmem_limit_bytes=VMEM_LIMIT),
        name="outproj_xattn",
    )(x, *ys, wout, gx, wq, kmem, vmem, wo, gf)


def _rows8(*rows):
    rows = [r.reshape(r.shape[0], 1, -1).astype(F32) for r in rows]
    depth, _, width = rows[0].shape
    return jnp.concatenate(rows + [jnp.zeros((depth, 8 - len(rows), width), F32)], axis=1)


def kernel(x, mem, norm_mix, w_in, rwkv_mu, rwkv_w0, rwkv_w_up, rwkv_a0, rwkv_a_up, rwkv_k_k, rwkv_k_a,
           rwkv_r_k, rwkv_ln_g, rwkv_ln_b, conv_w, conv_b, conv_ln_g, conv_ln_b, hgrn_lb_logits, hgrn_norm_g,
           w_out, norm_xattn, norm_mem, xattn_wq, xattn_wk, xattn_wv, xattn_wo, norm_final):
    b, t, d = x.shape
    depth = w_in.shape[0]
    n_shift = 3 * GROUP + 2 * RWKV_LORA
    n_rwkv = n_shift + GROUP
    lb_soft = jax.nn.softmax(hgrn_lb_logits.astype(F32), axis=0)
    lower_bounds = jnp.cumsum(lb_soft, axis=0) - lb_soft[0]
    mem2d = mem.reshape(-1, d)
    pad_lora = RWKV_SHIFT_COLS - n_shift
    lora_rows = LANES - 2 * RWKV_LORA

    zeros = lambda *shape: jnp.zeros((depth,) + shape, F32)
    wa = jnp.concatenate([w_in[:, :, :n_shift], zeros(d, pad_lora), w_in[:, :, n_shift:n_rwkv]], axis=2).astype(BF16)
    wb = w_in[:, :, n_rwkv:n_rwkv + 3 * GROUP].astype(BF16)
    wc = w_in[:, :, n_rwkv + 3 * GROUP:n_rwkv + 7 * GROUP].astype(BF16)
    sb0 = n_rwkv + 7 * GROUP
    wqg = jnp.concatenate([w_in[:, :, sb0:sb0 + GROUP], w_in[:, :, sb0 + 3 * GROUP:]], axis=2).astype(BF16)
    wkv = w_in[:, :, sb0 + GROUP:sb0 + 3 * GROUP].astype(BF16)
    g_mix = norm_mix.reshape(depth, 1, d)
    mu = jnp.concatenate([rwkv_mu, zeros(pad_lora)], axis=1).reshape(depth, 1, -1)
    rvec = _rows8(rwkv_w0, rwkv_a0, rwkv_k_k, rwkv_k_a, rwkv_r_k, rwkv_ln_g, rwkv_ln_b)
    wup = jnp.concatenate([rwkv_w_up, zeros(RWKV_LORA + lora_rows, GROUP)], axis=1)
    aup = jnp.concatenate([zeros(RWKV_LORA, GROUP), rwkv_a_up, zeros(lora_rows, GROUP)], axis=1)
    cw = jnp.concatenate([conv_w, zeros(32 - CONV_WIDTH, GROUP)], axis=1)
    cvec = _rows8(conv_b, conv_ln_g, conv_ln_b)
    hvec = _rows8(lower_bounds, hgrn_norm_g)
    g_mem = norm_mem.reshape(depth, 1, d)
    g_x = norm_xattn.reshape(depth, 1, d)
    wk, wv, wq, wo, wout = (a.astype(BF16) for a in (xattn_wk, xattn_wv, xattn_wq, xattn_wo, w_out))

    for l in range(depth):
        pa, pqg, pkv, y_b, y_c = _mixin(x, g_mix, wa, wb, wc, wqg, wkv, cw, cvec, hvec, l)
        y_a = _rwkv(pa, mu, rvec, wup, aup, l)
        y_d = _sb(pqg, pkv)
        kmem, vmem = _memkv(mem2d, g_mem, wk, wv, l)
        x = _post(x, (y_a, y_b, y_c, y_d), wout, g_x, wq, kmem.reshape(b, -1, d), vmem.reshape(b, -1, d), wo,
                  norm_final.reshape(1, d), l, final=(l == depth - 1))
    return x
```

```python
import functools

import jax
import jax.numpy as jnp
from jax import lax
from jax.experimental import pallas as pl
from jax.experimental.pallas import tpu as pltpu

F32 = jnp.float32
BF16 = jnp.bfloat16

GROUP = 256
N_HEADS = 4
HEAD_DIM = GROUP // N_HEADS
RWKV_LORA = 32
CONV_WIDTH = 31
XATTN_HEADS = 4
RMS_EPS = 1e-6
LN_EPS = 1e-5
RWKV_LN_EPS = 64e-5

LANES = 128
RWKV_SHIFT_COLS = 3 * GROUP + LANES
RWKV_COLS = RWKV_SHIFT_COLS + GROUP

CHUNK = 64
RWKV_TILE = 4 * CHUNK
SUB = 16
HGRN_PASSES = 1
HGRN_DIAG_PIECES = 1
SB_BLOCK = 128
SB_QBLOCKS = 2
SB_CUMSUM_PIECES = 2
CONV_HALO = 32
ROW_TILE = 256
MIXIN_FILL_COLS = 256
POST_ROW_TILE = 512
SB_LOG_UNDERFLOW = -120.0
NEG_BIG = -1e30
VMEM_LIMIT = 56 * 1024 * 1024

NN = ((1,), (0,))
NT = ((1,), (1,))
TN = ((0,), (0,))


def _pieces(x, n):
    out = []
    for _ in range(n - 1):
        p = x.astype(BF16)
        out.append(p)
        x = x - p.astype(F32)
    out.append(x.astype(BF16))
    return out


def _dg(a, b, dims=NN):
    return lax.dot_general(a, b, (dims, ((), ())), preferred_element_type=F32)


def _mm(a, b, dims=NN, passes=1):
    if passes == 1:
        return _dg(a.astype(BF16), b.astype(BF16), dims)
    a_hi, a_lo = _pieces(a, 2)
    b_hi, b_lo = _pieces(b, 2)
    return _dg(jnp.concatenate([a_hi, a_hi, a_lo], axis=dims[0][0]),
               jnp.concatenate([b_hi, b_lo, b_hi], axis=dims[1][0]), dims)


def _mm_exact_rhs(x, w_bf16, n=3):
    return _dg(jnp.concatenate(_pieces(x, n), axis=1), jnp.concatenate([w_bf16] * n, axis=0))


def _mm_exact_lhs(w_bf16, x, n=3):
    return _dg(jnp.concatenate([w_bf16] * n, axis=1), jnp.concatenate(_pieces(x, n), axis=0))


def _iota2(shape, dim):
    return lax.broadcasted_iota(jnp.int32, shape, dim)


def _log2(n):
    assert n & (n - 1) == 0
    return n.bit_length() - 1


HEAD_SHIFT = _log2(HEAD_DIM)


def _same_head_ones(n=GROUP):
    r = _iota2((n, n), 0) >> HEAD_SHIFT
    c = _iota2((n, n), 1) >> HEAD_SHIFT
    return jnp.where(r == c, 1.0, 0.0).astype(BF16)


def _lower_ones(n):
    return jnp.where(_iota2((n, n), 1) <= _iota2((n, n), 0), 1.0, 0.0).astype(BF16)


def _stack_heads(x):
    head = _iota2(x.shape, 1) >> HEAD_SHIFT
    return jnp.concatenate([jnp.where(head == h, x, 0.0) for h in range(N_HEADS)], axis=0)


PAIR_HEADS = LANES // HEAD_DIM


def _stack_pair(x, g):
    xg = x[:, g * LANES:(g + 1) * LANES]
    head = _iota2(xg.shape, 1) >> HEAD_SHIFT
    return jnp.concatenate([jnp.where(head == h, xg, 0.0) for h in range(PAIR_HEADS)], axis=0)


def _unstack_heads(y, t):
    return functools.reduce(lambda s, u: s + u, [y[h * t:(h + 1) * t] for h in range(N_HEADS)])


def _own_head_lanes(t):
    return (_iota2((N_HEADS * t, GROUP), 1) >> HEAD_SHIFT) == (_iota2((N_HEADS * t, GROUP), 0) >> _log2(t))


def _sigmoid(x):
    return 1.0 / (1.0 + jnp.exp(-x))


def _silu(x):
    return x * _sigmoid(x)


def _softplus(x):
    return jnp.maximum(x, 0.0) + jnp.log(1.0 + jnp.exp(-jnp.abs(x)))


def _rms_scale(x):
    return x * lax.rsqrt(jnp.mean(x * x, axis=-1, keepdims=True) + RMS_EPS)


def _layer(arr, l):
    shape = arr.shape[1:]
    return pl.BlockSpec((None,) + shape, lambda *_: (l,) + (0,) * len(shape))


def _rwkv_rows(p, prev_row, mu, vec, wup, aup):
    row = _iota2(p.shape, 0)
    p_prev = jnp.where(row == 0, prev_row, pltpu.roll(p, 1, 0))
    m = p + (p_prev - p) * mu
    r = m[:, 0:GROUP]
    k = m[:, GROUP:2 * GROUP]
    v = m[:, 2 * GROUP:3 * GROUP]
    lora = m[:, 3 * GROUP:RWKV_SHIFT_COLS]
    w0, a0, k_k, k_a, r_k = vec[0:1, :], vec[1:2, :], vec[2:3, :], vec[3:4, :], vec[4:5, :]
    gseg = _same_head_ones()
    w_log = -_softplus(-(w0 + _mm(jnp.tanh(lora), wup, NN, 3))) - 0.5
    lw = -jnp.exp(w_log)
    a_sig = _sigmoid(a0 + _mm(lora, aup, NN, 3))
    kk = k * k_k
    kk = kk / jnp.maximum(jnp.sqrt(_mm_exact_rhs(kk * kk, gseg)), 1e-12)
    k2 = k * (1.0 + (a_sig - 1.0) * k_a)
    bonus = _mm_exact_rhs(r * k2 * r_k, gseg) * v
    return r, k2, v, kk, a_sig, lw, bonus


def _rwkv_prep(chains):
    c = CHUNK
    n = PAIR_HEADS * c
    ltri = _lower_ones(c)
    ri = _iota2((n, n), 0)
    ci = _iota2((n, n), 1)
    cshift = _log2(c)
    same = (ri >> cshift) == (ci >> cshift)
    strict = same & (ci < ri)
    incl = same & (ci <= ri)
    fs = []
    for r, k2, v, kk, a_sig, lw in chains:
        cum = _mm_exact_lhs(ltri, lw)
        cum_last = cum[c - 1:c, :]
        g_inv = jnp.exp(-cum)
        g_rem = jnp.exp(cum_last - cum)
        a_dec = -kk * jnp.exp(cum - lw)
        r_dec = r * jnp.exp(cum)
        b = kk * a_sig
        decay = jnp.exp(cum_last)
        for g in range(N_HEADS // PAIR_HEADS):
            fs.append(dict(a_t=_stack_pair(a_dec, g), r_t=_stack_pair(r_dec, g),
                           b_t=_stack_pair(b * g_inv, g), k_t=_stack_pair(k2 * g_inv, g),
                           b_h=_stack_pair(b * g_rem, g), k_h=_stack_pair(k2 * g_rem, g),
                           v_s=_stack_pair(v, g), decay=decay[:, g * LANES:(g + 1) * LANES]))
    pairs = [_dg(jnp.concatenate([f["a_t"], f["r_t"]], axis=0).astype(BF16),
                 jnp.concatenate([f["b_t"], f["k_t"]], axis=0).astype(BF16), NT) for f in fs]
    for f, m in zip(fs, pairs):
        f["a_ab"] = jnp.where(strict, m[0:n, 0:n], 0.0)
        f["a_ak"] = jnp.where(strict, m[0:n, n:2 * n], 0.0)
        f["a_rb"] = jnp.where(incl, m[n:2 * n, 0:n], 0.0)
        f["a_rk"] = jnp.where(incl, m[n:2 * n, n:2 * n], 0.0)

    eye = jnp.where(ri == ci, 1.0, 0.0)
    for f in fs:
        f["tinv"] = eye + jnp.where((ri >> 1) == (ci >> 1), f["a_ab"], 0.0)
    for s in range(2, cshift + 1):
        lower_left = ((ri >> s) == (ci >> s)) & ((ri >> (s - 1)) != (ci >> (s - 1)))
        xs = [_mm(jnp.where(lower_left, f["a_ab"], 0.0), f["tinv"]) for f in fs]
        xs = [_mm(f["tinv"], x) for f, x in zip(fs, xs)]
        for f, x in zip(fs, xs):
            f["tinv"] = f["tinv"] + x
    for f in fs:
        f["w_m"] = _mm(f["tinv"], f["a_t"])
    xs = [_mm(f["a_ak"], f["v_s"]) for f in fs]
    for f, x in zip(fs, xs):
        f["z_m"] = _mm(f["tinv"], x)
    for f in fs:
        f["a_r"] = jnp.concatenate([f["a_rb"], f["a_rk"]], axis=1).astype(BF16)
        f["bk_h"] = jnp.concatenate([f["b_h"], f["k_h"]], axis=0).astype(BF16)
        f["wr"] = jnp.concatenate([f["w_m"], f["r_t"]], axis=0).astype(BF16)
        f["v_s"] = f["v_s"].astype(BF16)
    return fs


def _rwkv_step(fs, states):
    n = PAIR_HEADS * CHUNK
    sb = [s.astype(BF16) for s in states]
    wrs = [_dg(f["wr"], s, NT) for f, s in zip(fs, sb)]
    uvs = [jnp.concatenate([(wr[0:n] + f["z_m"]).astype(BF16), f["v_s"]], axis=0) for f, wr in zip(fs, wrs)]
    ys = [wr[n:2 * n] + _dg(f["a_r"], uv) for f, wr, uv in zip(fs, wrs, uvs)]
    new = [s * f["decay"] + _dg(uv, f["bk_h"], TN) for f, uv, s in zip(fs, uvs, states)]
    return ys, new


def _rwkv_kernel(p_ref, mu_ref, vec_ref, wup_ref, aup_ref, o_ref, s_ref, prev_ref):
    @pl.when(pl.program_id(0) == 0)
    def _():
        s_ref[...] = jnp.zeros_like(s_ref)
        prev_ref[...] = jnp.zeros_like(prev_ref)

    c = CHUNK
    nb, tile = p_ref.shape[0], p_ref.shape[1]
    vec = vec_ref[...]
    ln_g, ln_b = vec[5:6, :], vec[6:7, :]
    gseg = _same_head_ones()
    nch = tile // c
    rows = []
    for bi in range(nb):
        p = p_ref[bi, :, 0:RWKV_SHIFT_COLS]
        rows.append(_rwkv_rows(p, prev_ref[bi, 0:1, :], mu_ref[...], vec, wup_ref[...], aup_ref[...]))
        prev_ref[bi, 0:1, :] = p[tile - 1:tile, :]
    npair = N_HEADS // PAIR_HEADS
    factors = _rwkv_prep([tuple(a[j * c:(j + 1) * c] for a in rows[bi][:6]) for j in range(nch) for bi in range(nb)])
    states = [s_ref[bi, g] for bi in range(nb) for g in range(npair)]
    y_parts = [[] for _ in range(nb)]
    for j in range(nch):
        ys, states = _rwkv_step(factors[j * nb * npair:(j + 1) * nb * npair], states)
        for bi in range(nb):
            pair_ys = ys[bi * npair:(bi + 1) * npair]
            y_parts[bi].append(jnp.concatenate(
                [functools.reduce(lambda s, t: s + t, [y[h * c:(h + 1) * c] for h in range(PAIR_HEADS)])
                 for y in pair_ys], axis=1))
    for bi in range(nb):
        for g in range(npair):
            s_ref[bi, g] = states[bi * npair + g]
        y = jnp.concatenate(y_parts[bi], axis=0)
        mean = _mm_exact_rhs(y, gseg) * (1.0 / HEAD_DIM)
        yc = y - mean
        var = _mm_exact_rhs(yc * yc, gseg) * (1.0 / HEAD_DIM)
        yn = yc * lax.rsqrt(var + RWKV_LN_EPS) * ln_g + ln_b
        gate = p_ref[bi, :, RWKV_SHIFT_COLS:RWKV_COLS]
        o_ref[bi] = ((yn + rows[bi][6]) * _silu(gate)).astype(o_ref.dtype)


def _rwkv(pa, mu, vec, wup, aup, l):
    b, t, _ = pa.shape
    return pl.pallas_call(
        _rwkv_kernel,
        grid=(t // RWKV_TILE,),
        in_specs=[pl.BlockSpec((b, RWKV_TILE, RWKV_COLS), lambda j: (0, j, 0))]
        + [_layer(a, l) for a in (mu, vec, wup, aup)],
        out_specs=pl.BlockSpec((b, RWKV_TILE, GROUP), lambda j: (0, j, 0)),
        out_shape=jax.ShapeDtypeStruct((b, t, GROUP), BF16),
        scratch_shapes=[pltpu.VMEM((b, N_HEADS // PAIR_HEADS, LANES, LANES), F32),
                        pltpu.VMEM((b, 8, RWKV_SHIFT_COLS), F32)],
        compiler_params=pltpu.CompilerParams(dimension_semantics=("arbitrary",), vmem_limit_bytes=VMEM_LIMIT),
        name="rwkv7",
    )(pa, mu, vec, wup, aup)


def _conv_body(p_ref, w_ref, vec_ref, o_ref, u_ref, sh_ref):
    tt = p_ref.shape[0]
    sub = 8
    u_ref[CONV_HALO:CONV_HALO + tt, :] = p_ref[:, 0:GROUP] * _sigmoid(p_ref[:, GROUP:2 * GROUP])
    base = CONV_HALO - (CONV_WIDTH - 1)
    rows = sh_ref.shape[1]
    for b in range(1, sub):
        sh_ref[b - 1] = u_ref[b:b + rows, :]
    acc = jnp.zeros((tt, GROUP), F32)
    for j in range(CONV_WIDTH):
        a, b = divmod(base + j, sub)
        src = u_ref[a * sub:a * sub + tt, :] if b == 0 else sh_ref[b - 1, a * sub:a * sub + tt, :]
        acc = acc + src * w_ref[j:j + 1, :]
    u_ref[0:CONV_HALO, :] = u_ref[tt:tt + CONV_HALO, :]
    y = acc + vec_ref[0:1, :]
    mu = jnp.mean(y, axis=-1, keepdims=True)
    yc = y - mu
    var = jnp.mean(yc * yc, axis=-1, keepdims=True)
    yn = yc * lax.rsqrt(var + LN_EPS) * vec_ref[1:2, :] + vec_ref[2:3, :]
    o_ref[...] = (_silu(yn) * _silu(p_ref[:, 2 * GROUP:3 * GROUP])).astype(o_ref.dtype)


def _hgrn_diag_rows(i, s):
    half = SUB // 2
    per_sub = SUB * half + half * half
    if s < half:
        return i * per_sub + s * SUB, SUB, 0
    return i * per_sub + half * SUB + (s - half) * half, half, half


def _hgrn_body(p_ref, vec_ref, o_ref, s_ref, b_ref, k_ref, v_ref, p_buf, r_buf, fillers=()):
    fillers = list(fillers)

    def fill():
        if fillers:
            fillers.pop(0)()

    c = CHUNK
    nb, tile = p_ref.shape[0], p_ref.shape[1]
    nch = tile // c
    nsub = c // SUB
    sshift = _log2(SUB)
    lb, norm_g = vec_ref[0:1, :], vec_ref[1:2, :]
    gseg = _same_head_ones()
    ltri = _lower_ones(c)
    chains = [(bi, j) for j in range(nch) for bi in range(nb)]
    q, kf, v, bcum = [], [], [], []
    for bi, j in chains:
        rows = slice(j * c, (j + 1) * c)
        q.append(_silu(p_ref[bi, rows, 0:GROUP]))
        f = lb + (1.0 - lb) * _sigmoid(p_ref[bi, rows, GROUP:2 * GROUP])
        kf.append(1.0 - f)
        v.append(p_ref[bi, rows, 2 * GROUP:3 * GROUP])
        bcum.append(jnp.log(f))
    bcum = [_mm_exact_lhs(ltri, lf) for lf in bcum]
    for n in range(len(chains)):
        b_ref[n] = bcum[n]
        k_ref[n] = kf[n]
        v_ref[n] = v[n]
    b_last = [b[c - 1:c, :] for b in bcum]
    same_head = (_iota2((GROUP, GROUP), 0) >> HEAD_SHIFT) == (_iota2((GROUP, GROUP), 1) >> HEAD_SHIFT)
    kv = [jnp.where(same_head, _mm(v[n], kf[n] * jnp.exp(b_last[n] - bcum[n]), TN, HGRN_PASSES), 0.0)
          for n in range(len(chains))]

    trow = _iota2((c, GROUP), 0)
    tsub = trow >> sshift
    q_off, k_cat = [], []
    for n in range(len(chains)):
        beta = b_ref[n, SUB - 1:SUB, :]
        for i in range(2, nsub):
            beta = jnp.where(tsub == i, b_ref[n, i * SUB - 1:i * SUB, :], beta)
        q_off.append(_stack_heads(q[n] * jnp.exp(jnp.where(tsub >= 1, bcum[n] - beta, NEG_BIG))))
        k_cat.append(jnp.concatenate(
            [kf[n] * jnp.exp(jnp.where(trow < i * SUB, b_ref[n, i * SUB - 1:i * SUB, :] - bcum[n], NEG_BIG))
             for i in range(1, nsub)], axis=0))
    att = [_mm(qo, kc, NT, HGRN_PASSES) for qo, kc in zip(q_off, k_cat)]
    arow = (_iota2(att[0].shape, 0) & (c - 1)) >> sshift
    acol = _iota2(att[0].shape, 1) >> _log2(c)
    att = [jnp.where(acol + 1 == arow, a, 0.0) for a in att]
    o_off = [_mm(a, jnp.concatenate([vn] * (nsub - 1), axis=0)) for a, vn in zip(att, v)]
    own = _own_head_lanes(c)
    o_off = [_unstack_heads(jnp.where(own, o, 0.0), c) for o in o_off]

    half = SUB // 2
    for n in range(len(chains)):
        for i in range(nsub):
            for s in range(SUB):
                src = i * SUB + s
                start, nrows, first = _hgrn_diag_rows(i, s)
                tgt = slice(i * SUB + first, (i + 1) * SUB)
                tau = _iota2((nrows, GROUP), 0) + first
                prod = (q[n][tgt] * jnp.exp(jnp.where(tau >= s, bcum[n][tgt] - b_ref[n, src:src + 1, :], NEG_BIG))
                        * k_ref[n, src:src + 1, :])
                p_buf[n, start:start + nrows, :] = prod
        fill()
    for n in range(len(chains)):
        r_buf[n] = _mm_exact_rhs(p_buf[n], gseg, HGRN_DIAG_PIECES)
    o_diag = []
    for n in range(len(chains)):
        fill()
        parts = []
        for i in range(nsub):
            lo = jnp.zeros((half, GROUP), F32)
            hi = jnp.zeros((half, GROUP), F32)
            for s in range(SUB):
                src = i * SUB + s
                start, nrows, first = _hgrn_diag_rows(i, s)
                vs = v_ref[n, src:src + 1, :]
                if first == 0:
                    lo = lo + r_buf[n, start:start + half, :] * vs
                    hi = hi + r_buf[n, start + half:start + SUB, :] * vs
                else:
                    hi = hi + r_buf[n, start:start + half, :] * vs
            parts += [lo, hi]
        o_diag.append(jnp.concatenate(parts, axis=0))

    states = [s_ref[bi] for bi in range(nb)]
    for j in range(nch):
        ns = range(j * nb, (j + 1) * nb)
        o_inter = [_mm(q[n] * jnp.exp(bcum[n]), states[n - j * nb], NT, HGRN_PASSES) for n in ns]
        states = [states[n - j * nb] * jnp.exp(b_last[n]) + kv[n] for n in ns]
        for n, oi in zip(ns, o_inter):
            bi = n - j * nb
            o = oi + o_off[n] + o_diag[n]
            ms = _mm_exact_rhs(o * o, gseg) * (1.0 / HEAD_DIM)
            o = o * lax.rsqrt(ms + RMS_EPS) * norm_g
            gate = p_ref[bi, j * c:(j + 1) * c, 3 * GROUP:4 * GROUP]
            o_ref[bi, j * c:(j + 1) * c, :] = (o * _silu(gate)).astype(o_ref.dtype)
    for bi in range(nb):
        s_ref[bi] = states[bi]
    while fillers:
        fill()


def _mixin_kernel(x_ref, g_ref, wa_ref, wb_ref, wc_ref, wqg_ref, wkv_ref, cw_ref, cvec_ref, hvec_ref,
                  pa_ref, pqg_ref, pkv_ref, yb_ref, yc_ref,
                  pb_s, pc_s, u_ref, sh_ref, s_ref, b_ref, k_ref, v_ref, p_buf, r_buf):
    @pl.when(pl.program_id(1) == 0)
    def _():
        u_ref[0:CONV_HALO, :] = jnp.zeros((CONV_HALO, GROUP), F32)
        s_ref[...] = jnp.zeros_like(s_ref)

    hb = (_rms_scale(x_ref[...]) * g_ref[...]).astype(BF16)
    pb_s[...] = _dg(hb, wb_ref[...])
    pc_s[0] = _dg(hb, wc_ref[...])

    def project(w_ref, o_ref, c0, c1):
        def run():
            o_ref[:, c0:c1] = _dg(hb, w_ref[:, c0:c1]).astype(o_ref.dtype)
        return run

    pieces = [project(w_ref, o_ref, c0, min(c0 + MIXIN_FILL_COLS, w_ref.shape[1]))
              for w_ref, o_ref in ((wa_ref, pa_ref), (wqg_ref, pqg_ref), (wkv_ref, pkv_ref))
              for c0 in range(0, w_ref.shape[1], MIXIN_FILL_COLS)]
    pieces[0]()
    _conv_body(pb_s, cw_ref, cvec_ref, yb_ref, u_ref, sh_ref)
    _hgrn_body(pc_s, hvec_ref, yc_ref, s_ref, b_ref, k_ref, v_ref, p_buf, r_buf, fillers=pieces[1:])


def _mixin(x, g, wa, wb, wc, wqg, wkv, cw, cvec, hvec, l):
    b, t, d = x.shape
    tm = ROW_TILE
    nchains = tm // CHUNK
    diag_rows = (CHUNK // SUB) * (SUB * SUB // 2 + SUB * SUB // 4)
    rowblk = lambda w: pl.BlockSpec((None, tm, w), lambda i, j: (i, j, 0))
    widths = (wa.shape[-1], wqg.shape[-1], wkv.shape[-1])
    return pl.pallas_call(
        _mixin_kernel,
        grid=(b, t // tm),
        in_specs=[rowblk(d)] + [_layer(a, l) for a in (g, wa, wb, wc, wqg, wkv, cw, cvec, hvec)],
        out_specs=[rowblk(w) for w in widths] + [rowblk(GROUP), pl.BlockSpec((1, tm, GROUP), lambda i, j: (i, j, 0))],
        out_shape=[jax.ShapeDtypeStruct((b, t, w), dt) for w, dt in zip(widths, (F32, F32, BF16))]
        + [jax.ShapeDtypeStruct((b, t, GROUP), BF16)] * 2,
        scratch_shapes=[pltpu.VMEM((tm, wb.shape[-1]), F32), pltpu.VMEM((1, tm, wc.shape[-1]), F32),
                        pltpu.VMEM((CONV_HALO + tm, GROUP), F32), pltpu.VMEM((7, CONV_HALO + tm - 8, GROUP), F32),
                        pltpu.VMEM((1, GROUP, GROUP), F32)] + [pltpu.VMEM((nchains, CHUNK, GROUP), F32)] * 3
        + [pltpu.VMEM((nchains, diag_rows, GROUP), F32)] * 2,
        compiler_params=pltpu.CompilerParams(dimension_semantics=("arbitrary", "arbitrary"),
                                             vmem_limit_bytes=VMEM_LIMIT),
        name="inproj_conv_hgrn",
    )(x, g, wa, wb, wc, wqg, wkv, cw, cvec, hvec)


def _sb_kernel(q_ref, k_ref, v_ref, gate_ref, o_ref, acc_ref, carry_ref, vst_ref):
    tq = SB_BLOCK
    nq = SB_QBLOCKS
    n = N_HEADS * tq
    first_qb = pl.program_id(1) * nq

    @pl.when(pl.program_id(1) == 0)
    def _():
        def fill(j, c):
            vst_ref[j] = _stack_heads(v_ref[pl.ds(pl.multiple_of(j * tq, tq), tq), :])
            return c
        lax.fori_loop(0, v_ref.shape[0] // tq, fill, 0)

    qs = [_stack_heads(q_ref[i * tq:(i + 1) * tq, :] * (HEAD_DIM ** -0.5)).astype(BF16) for i in range(nq)]
    acc_ref[...] = jnp.zeros_like(acc_ref)
    carry_ref[...] = jnp.zeros_like(carry_ref)

    row = _iota2((n, tq), 0) & (tq - 1)
    col = _iota2((n, tq), 1)
    cs_mat = jnp.where(_iota2((tq, tq), 0) > _iota2((tq, tq), 1), 1.0, 0.0).astype(BF16)

    def body(state):
        it, _ = state
        js = [first_qb + i - it for i in range(nq)]
        jcs = [jnp.maximum(j, 0) for j in js]
        zs = [_dg(qs[i], k_ref[pl.ds(pl.multiple_of(jcs[i] * tq, tq), tq), :], NT) for i in range(nq)]
        log_keeps = [-_softplus(z) for z in zs]
        masks = [col < row + jnp.where(j >= 0, it * tq, -tq) for j in js]
        lkms = [jnp.where(m, lk, 0.0) for m, lk in zip(masks, log_keeps)]
        css = [_mm_exact_rhs(lkm, cs_mat, SB_CUMSUM_PIECES) for lkm in lkms]
        atts = []
        more = jnp.bool_(False)
        for i in range(nq):
            carry = carry_ref[i]
            att = jnp.where(masks[i], jnp.exp(zs[i] + log_keeps[i] + css[i] + carry), 0.0).astype(BF16)
            atts.append(jnp.concatenate([att[h * tq:(h + 1) * tq] for h in range(N_HEADS)], axis=1))
            carry = carry + jnp.sum(lkms[i], axis=1, keepdims=True)
            carry_ref[i] = carry
            more = more | ((js[i] > 0) & (jnp.max(carry) > SB_LOG_UNDERFLOW))
        avs = [_dg(atts[i], vst_ref[jcs[i]]) for i in range(nq)]
        for i in range(nq):
            acc_ref[i] += avs[i]
        return it + 1, more.astype(jnp.int32)

    lax.while_loop(lambda s: s[1] > 0, body, (jnp.int32(0), jnp.int32(1)))
    for i in range(nq):
        o_ref[i * tq:(i + 1) * tq, :] = (acc_ref[i] * _silu(gate_ref[i * tq:(i + 1) * tq, :])).astype(o_ref.dtype)


def _sb(pqg, pkv):
    b, t, _ = pqg.shape
    tq = SB_BLOCK
    nq = SB_QBLOCKS
    blk = lambda cidx: pl.BlockSpec((None, nq * tq, GROUP), lambda i, j: (i, j, cidx))
    full = lambda cidx: pl.BlockSpec((None, t, GROUP), lambda i, j: (i, 0, cidx))
    return pl.pallas_call(
        _sb_kernel,
        grid=(b, t // (nq * tq)),
        in_specs=[blk(0), full(0), full(1), blk(1)],
        out_specs=pl.BlockSpec((None, nq * tq, GROUP), lambda i, j: (i, j, 0)),
        out_shape=jax.ShapeDtypeStruct((b, t, GROUP), BF16),
        scratch_shapes=[pltpu.VMEM((nq, tq, GROUP), F32), pltpu.VMEM((nq, N_HEADS * tq, tq), F32),
                        pltpu.VMEM((t // tq, N_HEADS * tq, GROUP), BF16)],
        compiler_params=pltpu.CompilerParams(dimension_semantics=("arbitrary", "arbitrary"),
                                             vmem_limit_bytes=VMEM_LIMIT),
        name="stick_breaking",
    )(pqg, pkv, pkv, pqg)


def _memkv_kernel(m_ref, g_ref, wk_ref, wv_ref, k_ref, v_ref):
    h = (_rms_scale(m_ref[...]) * g_ref[...]).astype(BF16)
    k_ref[...] = _dg(h, wk_ref[...]).astype(k_ref.dtype)
    v_ref[...] = _dg(h, wv_ref[...]).astype(v_ref.dtype)


def _memkv(mem2d, g, wk, wv, l):
    n, d = mem2d.shape
    tm = ROW_TILE
    row = lambda i: (i, 0)
    return pl.pallas_call(
        _memkv_kernel,
        grid=(n // tm,),
        in_specs=[pl.BlockSpec((tm, d), row), _layer(g, l), _layer(wk, l), _layer(wv, l)],
        out_specs=[pl.BlockSpec((tm, d), row)] * 2,
        out_shape=[jax.ShapeDtypeStruct((n, d), BF16)] * 2,
        compiler_params=pltpu.CompilerParams(dimension_semantics=("arbitrary",), vmem_limit_bytes=VMEM_LIMIT),
        name="mem_kv",
    )(mem2d, g, wk, wv)


def _post_kernel(x_ref, ya_ref, yb_ref, yc_ref, yd_ref, wout_ref, gx_ref, wq_ref, k_ref, v_ref, wo_ref,
                 gf_ref, o_ref, *, final):
    d = x_ref.shape[-1]
    hd = d // XATTN_HEADS
    x1 = x_ref[...]
    for g, y_ref in enumerate((ya_ref, yb_ref, yc_ref, yd_ref)):
        x1 = x1 + _dg(y_ref[...], wout_ref[g * GROUP:(g + 1) * GROUP, :])
    hx = (_rms_scale(x1) * gx_ref[...]).astype(BF16)
    q = _dg(hx, wq_ref[...])
    outs = []
    for h in range(XATTN_HEADS):
        sl = slice(h * hd, (h + 1) * hd)
        s = _dg(q[:, sl].astype(BF16), k_ref[:, sl], NT) * (hd ** -0.5)
        s = s - jnp.max(s, axis=-1, keepdims=True)
        e = jnp.exp(s)
        p = e / jnp.sum(e, axis=-1, keepdims=True)
        outs.append(_dg(p.astype(BF16), v_ref[:, sl]).astype(BF16))
    x2 = x1 + _dg(jnp.concatenate(outs, axis=-1), wo_ref[...])
    if final:
        x2 = _rms_scale(x2) * gf_ref[...]
    o_ref[...] = x2


def _post(x, ys, wout, gx, wq, kmem, vmem, wo, gf, l, final):
    b, t, d = x.shape
    m = kmem.shape[1]
    tm = POST_ROW_TILE
    rowblk = lambda w: pl.BlockSpec((None, tm, w), lambda i, j: (i, j, 0))
    return pl.pallas_call(
        functools.partial(_post_kernel, final=final),
        grid=(b, t // tm),
        in_specs=[rowblk(d)] + [rowblk(GROUP)] * 4
        + [_layer(wout, l), _layer(gx, l), _layer(wq, l),
           pl.BlockSpec((None, m, d), lambda i, j: (i, 0, 0)), pl.BlockSpec((None, m, d), lambda i, j: (i, 0, 0)),
           _layer(wo, l), pl.BlockSpec((1, d), lambda i, j: (0, 0))],
        out_specs=rowblk(d),
        out_shape=jax.ShapeDtypeStruct((b, t, d), F32),
        compiler_params=pltpu.CompilerParams(dimension_semantics=("arbitrary", "arbitrary"),
                                             vmem_limit_bytes=VMEM_LIMIT),
        name="outproj_xattn",
    )(x, *ys, wout, gx, wq, kmem, vmem, wo, gf)


def _rows8(*rows):
    rows = [r.reshape(r.shape[0], 1, -1).astype(F32) for r in rows]
    depth, _, width = rows[0].shape
    return jnp.concatenate(rows + [jnp.zeros((depth, 8 - len(rows), width), F32)], axis=1)


def kernel(x, mem, norm_mix, w_in, rwkv_mu, rwkv_w0, rwkv_w_up, rwkv_a0, rwkv_a_up, rwkv_k_k, rwkv_k_a,
           rwkv_r_k, rwkv_ln_g, rwkv_ln_b, conv_w, conv_b, conv_ln_g, conv_ln_b, hgrn_lb_logits, hgrn_norm_g,
           w_out, norm_xattn, norm_mem, xattn_wq, xattn_wk, xattn_wv, xattn_wo, norm_final):
    b, t, d = x.shape
    depth = w_in.shape[0]
    n_shift = 3 * GROUP + 2 * RWKV_LORA
    n_rwkv = n_shift + GROUP
    lb_soft = jax.nn.softmax(hgrn_lb_logits.astype(F32), axis=0)
    lower_bounds = jnp.cumsum(lb_soft, axis=0) - lb_soft[0]
    mem2d = mem.reshape(-1, d)
    pad_lora = RWKV_SHIFT_COLS - n_shift
    lora_rows = LANES - 2 * RWKV_LORA

    zeros = lambda *shape: jnp.zeros((depth,) + shape, F32)
    wa = jnp.concatenate([w_in[:, :, :n_shift], zeros(d, pad_lora), w_in[:, :, n_shift:n_rwkv]], axis=2).astype(BF16)
    wb = w_in[:, :, n_rwkv:n_rwkv + 3 * GROUP].astype(BF16)
    wc = w_in[:, :, n_rwkv + 3 * GROUP:n_rwkv + 7 * GROUP].astype(BF16)
    sb0 = n_rwkv + 7 * GROUP
    wqg = jnp.concatenate([w_in[:, :, sb0:sb0 + GROUP], w_in[:, :, sb0 + 3 * GROUP:]], axis=2).astype(BF16)
    wkv = w_in[:, :, sb0 + GROUP:sb0 + 3 * GROUP].astype(BF16)
    g_mix = norm_mix.reshape(depth, 1, d)
    mu = jnp.concatenate([rwkv_mu, zeros(pad_lora)], axis=1).reshape(depth, 1, -1)
    rvec = _rows8(rwkv_w0, rwkv_a0, rwkv_k_k, rwkv_k_a, rwkv_r_k, rwkv_ln_g, rwkv_ln_b)
    wup = jnp.concatenate([rwkv_w_up, zeros(RWKV_LORA + lora_rows, GROUP)], axis=1)
    aup = jnp.concatenate([zeros(RWKV_LORA, GROUP), rwkv_a_up, zeros(lora_rows, GROUP)], axis=1)
    cw = jnp.concatenate([conv_w, zeros(32 - CONV_WIDTH, GROUP)], axis=1)
    cvec = _rows8(conv_b, conv_ln_g, conv_ln_b)
    hvec = _rows8(lower_bounds, hgrn_norm_g)
    g_mem = norm_mem.reshape(depth, 1, d)
    g_x = norm_xattn.reshape(depth, 1, d)
    wk, wv, wq, wo, wout = (a.astype(BF16) for a in (xattn_wk, xattn_wv, xattn_wq, xattn_wo, w_out))

    for l in range(depth):
        pa, pqg, pkv, y_b, y_c = _mixin(x, g_mix, wa, wb, wc, wqg, wkv, cw, cvec, hvec, l)
        y_a = _rwkv(pa, mu, rvec, wup, aup, l)
        y_d = _sb(pqg, pkv)
        kmem, vmem = _memkv(mem2d, g_mem, wk, wv, l)
        x = _post(x, (y_a, y_b, y_c, y_d), wout, g_x, wq, kmem.reshape(b, -1, d), vmem.reshape(b, -1, d), wo,
                  norm_final.reshape(1, d), l, final=(l == depth - 1))
    return x
```

```python
import functools

import jax
import jax.numpy as jnp
from jax import lax
from jax.experimental import pallas as pl
from jax.experimental.pallas import tpu as pltpu

F32 = jnp.float32
BF16 = jnp.bfloat16

GROUP = 256
N_HEADS = 4
HEAD_DIM = GROUP // N_HEADS
RWKV_LORA = 32
CONV_WIDTH = 31
XATTN_HEADS = 4
RMS_EPS = 1e-6
LN_EPS = 1e-5
RWKV_LN_EPS = 64e-5

LANES = 128
RWKV_SHIFT_COLS = 3 * GROUP + LANES
RWKV_COLS = RWKV_SHIFT_COLS + GROUP

CHUNK = 64
RWKV_TILE = 8 * CHUNK
SUB = 16
HGRN_PASSES = 1
HGRN_DIAG_PIECES = 1
SB_BLOCK = 128
SB_QBLOCKS = 4
SB_CUMSUM_PIECES = 2
CONV_HALO = 32
ROW_TILE = 512
MIXIN_FILL_COLS = 256
POST_ROW_TILE = 1024
SB_LOG_UNDERFLOW = -120.0
LOG2E = 1.4426950408889634
NEG_BIG = -1e30
VMEM_LIMIT = 56 * 1024 * 1024

NN = ((1,), (0,))
NT = ((1,), (1,))
TN = ((0,), (0,))


def _pieces(x, n):
    out = []
    for _ in range(n - 1):
        p = x.astype(BF16)
        out.append(p)
        x = x - p.astype(F32)
    out.append(x.astype(BF16))
    return out


def _dg(a, b, dims=NN):
    return lax.dot_general(a, b, (dims, ((), ())), preferred_element_type=F32)


def _mm(a, b, dims=NN, passes=1):
    if passes == 1:
        return _dg(a.astype(BF16), b.astype(BF16), dims)
    a_hi, a_lo = _pieces(a, 2)
    b_hi, b_lo = _pieces(b, 2)
    return _dg(jnp.concatenate([a_hi, a_hi, a_lo], axis=dims[0][0]),
               jnp.concatenate([b_hi, b_lo, b_hi], axis=dims[1][0]), dims)


def _mm_exact_rhs(x, w_bf16, n=3):
    return _dg(jnp.concatenate(_pieces(x, n), axis=1), jnp.concatenate([w_bf16] * n, axis=0))


def _mm_exact_lhs(w_bf16, x, n=3):
    return _dg(jnp.concatenate([w_bf16] * n, axis=1), jnp.concatenate(_pieces(x, n), axis=0))


def _iota2(shape, dim):
    return lax.broadcasted_iota(jnp.int32, shape, dim)


def _log2(n):
    assert n & (n - 1) == 0
    return n.bit_length() - 1


HEAD_SHIFT = _log2(HEAD_DIM)


def _same_head_ones(n=GROUP):
    r = _iota2((n, n), 0) >> HEAD_SHIFT
    c = _iota2((n, n), 1) >> HEAD_SHIFT
    return jnp.where(r == c, 1.0, 0.0).astype(BF16)


def _lower_ones(n):
    return jnp.where(_iota2((n, n), 1) <= _iota2((n, n), 0), 1.0, 0.0).astype(BF16)


def _stack_heads(x):
    head = _iota2(x.shape, 1) >> HEAD_SHIFT
    return jnp.concatenate([jnp.where(head == h, x, 0.0) for h in range(N_HEADS)], axis=0)


PAIR_HEADS = LANES // HEAD_DIM


def _stack_pair(x, g):
    xg = x[:, g * LANES:(g + 1) * LANES]
    head = _iota2(xg.shape, 1) >> HEAD_SHIFT
    return jnp.concatenate([jnp.where(head == h, xg, 0.0) for h in range(PAIR_HEADS)], axis=0)


def _unstack_heads(y, t):
    return functools.reduce(lambda s, u: s + u, [y[h * t:(h + 1) * t] for h in range(N_HEADS)])


def _own_head_lanes(t):
    return (_iota2((N_HEADS * t, GROUP), 1) >> HEAD_SHIFT) == (_iota2((N_HEADS * t, GROUP), 0) >> _log2(t))


def _sigmoid(x):
    return 1.0 / (1.0 + jnp.exp(-x))


def _silu(x):
    return x * _sigmoid(x)


def _softplus(x):
    return jnp.maximum(x, 0.0) + jnp.log(1.0 + jnp.exp(-jnp.abs(x)))


def _rms_scale(x):
    return x * lax.rsqrt(jnp.mean(x * x, axis=-1, keepdims=True) + RMS_EPS)


def _layer(arr, l):
    shape = arr.shape[1:]
    return pl.BlockSpec((None,) + shape, lambda *_: (l,) + (0,) * len(shape))


def _rwkv_rows(p, prev_row, mu, vec, wup, aup):
    row = _iota2(p.shape, 0)
    p_prev = jnp.where(row == 0, prev_row, pltpu.roll(p, 1, 0))
    m = p + (p_prev - p) * mu
    r = m[:, 0:GROUP]
    k = m[:, GROUP:2 * GROUP]
    v = m[:, 2 * GROUP:3 * GROUP]
    lora = m[:, 3 * GROUP:RWKV_SHIFT_COLS]
    w0, a0, k_k, k_a, r_k = vec[0:1, :], vec[1:2, :], vec[2:3, :], vec[3:4, :], vec[4:5, :]
    gseg = _same_head_ones()
    w_log = -_softplus(-(w0 + _mm(jnp.tanh(lora), wup, NN, 3))) - 0.5
    lw = -jnp.exp(w_log)
    a_sig = _sigmoid(a0 + _mm(lora, aup, NN, 3))
    kk = k * k_k
    kk = kk / jnp.maximum(jnp.sqrt(_mm_exact_rhs(kk * kk, gseg)), 1e-12)
    k2 = k * (1.0 + (a_sig - 1.0) * k_a)
    bonus = _mm_exact_rhs(r * k2 * r_k, gseg) * v
    return r, k2, v, kk, a_sig, lw, bonus


def _rwkv_prep(chains):
    c = CHUNK
    n = PAIR_HEADS * c
    ltri = _lower_ones(c)
    ri = _iota2((n, n), 0)
    ci = _iota2((n, n), 1)
    cshift = _log2(c)
    same = (ri >> cshift) == (ci >> cshift)
    strict = same & (ci < ri)
    incl = same & (ci <= ri)
    fs = []
    for r, k2, v, kk, a_sig, lw in chains:
        cum = _mm_exact_lhs(ltri, lw)
        cum_last = cum[c - 1:c, :]
        g_inv = jnp.exp(-cum)
        g_rem = jnp.exp(cum_last - cum)
        a_dec = -kk * jnp.exp(cum - lw)
        r_dec = r * jnp.exp(cum)
        b = kk * a_sig
        decay = jnp.exp(cum_last)
        for g in range(N_HEADS // PAIR_HEADS):
            fs.append(dict(a_t=_stack_pair(a_dec, g), r_t=_stack_pair(r_dec, g),
                           b_t=_stack_pair(b * g_inv, g), k_t=_stack_pair(k2 * g_inv, g),
                           b_h=_stack_pair(b * g_rem, g), k_h=_stack_pair(k2 * g_rem, g),
                           v_s=_stack_pair(v, g), decay=decay[:, g * LANES:(g + 1) * LANES]))
    pairs = [_dg(jnp.concatenate([f["a_t"], f["r_t"]], axis=0).astype(BF16),
                 jnp.concatenate([f["b_t"], f["k_t"]], axis=0).astype(BF16), NT) for f in fs]
    for f, m in zip(fs, pairs):
        f["a_ab"] = jnp.where(strict, m[0:n, 0:n], 0.0)
        f["a_ak"] = jnp.where(strict, m[0:n, n:2 * n], 0.0)
        f["a_rb"] = jnp.where(incl, m[n:2 * n, 0:n], 0.0)
        f["a_rk"] = jnp.where(incl, m[n:2 * n, n:2 * n], 0.0)

    eye = jnp.where(ri == ci, 1.0, 0.0)
    for f in fs:
        f["tinv"] = eye + jnp.where((ri >> 1) == (ci >> 1), f["a_ab"], 0.0)
    for s in range(2, cshift + 1):
        lower_left = ((ri >> s) == (ci >> s)) & ((ri >> (s - 1)) != (ci >> (s - 1)))
        xs = [_mm(jnp.where(lower_left, f["a_ab"], 0.0), f["tinv"]) for f in fs]
        xs = [_mm(f["tinv"], x) for f, x in zip(fs, xs)]
        for f, x in zip(fs, xs):
            f["tinv"] = f["tinv"] + x
    for f in fs:
        f["w_m"] = _mm(f["tinv"], f["a_t"])
    xs = [_mm(f["a_ak"], f["v_s"]) for f in fs]
    for f, x in zip(fs, xs):
        f["z_m"] = _mm(f["tinv"], x)
    for f in fs:
        f["a_r"] = jnp.concatenate([f["a_rb"], f["a_rk"]], axis=1).astype(BF16)
        f["bk_h"] = jnp.concatenate([f["b_h"], f["k_h"]], axis=0).astype(BF16)
        f["wr"] = jnp.concatenate([f["w_m"], f["r_t"]], axis=0).astype(BF16)
        f["v_s"] = f["v_s"].astype(BF16)
    return fs


def _rwkv_step(fs, states):
    n = PAIR_HEADS * CHUNK
    sb = [s.astype(BF16) for s in states]
    wrs = [_dg(f["wr"], s, NT) for f, s in zip(fs, sb)]
    uvs = [jnp.concatenate([(wr[0:n] + f["z_m"]).astype(BF16), f["v_s"]], axis=0) for f, wr in zip(fs, wrs)]
    ys = [wr[n:2 * n] + _dg(f["a_r"], uv) for f, wr, uv in zip(fs, wrs, uvs)]
    new = [s * f["decay"] + _dg(uv, f["bk_h"], TN) for f, uv, s in zip(fs, uvs, states)]
    return ys, new


def _rwkv_kernel(p_ref, mu_ref, vec_ref, wup_ref, aup_ref, o_ref, s_ref, prev_ref):
    @pl.when(pl.program_id(0) == 0)
    def _():
        s_ref[...] = jnp.zeros_like(s_ref)
        prev_ref[...] = jnp.zeros_like(prev_ref)

    c = CHUNK
    nb, tile = p_ref.shape[0], p_ref.shape[1]
    vec = vec_ref[...]
    ln_g, ln_b = vec[5:6, :], vec[6:7, :]
    gseg = _same_head_ones()
    nch = tile // c
    rows = []
    for bi in range(nb):
        p = p_ref[bi, :, 0:RWKV_SHIFT_COLS]
        rows.append(_rwkv_rows(p, prev_ref[bi, 0:1, :], mu_ref[...], vec, wup_ref[...], aup_ref[...]))
        prev_ref[bi, 0:1, :] = p[tile - 1:tile, :]
    npair = N_HEADS // PAIR_HEADS
    factors = _rwkv_prep([tuple(a[j * c:(j + 1) * c] for a in rows[bi][:6]) for j in range(nch) for bi in range(nb)])
    states = [s_ref[bi, g] for bi in range(nb) for g in range(npair)]
    y_parts = [[] for _ in range(nb)]
    for j in range(nch):
        ys, states = _rwkv_step(factors[j * nb * npair:(j + 1) * nb * npair], states)
        for bi in range(nb):
            pair_ys = ys[bi * npair:(bi + 1) * npair]
            y_parts[bi].append(jnp.concatenate(
                [functools.reduce(lambda s, t: s + t, [y[h * c:(h + 1) * c] for h in range(PAIR_HEADS)])
                 for y in pair_ys], axis=1))
    for bi in range(nb):
        for g in range(npair):
            s_ref[bi, g] = states[bi * npair + g]
        y = jnp.concatenate(y_parts[bi], axis=0)
        mean = _mm_exact_rhs(y, gseg) * (1.0 / HEAD_DIM)
        yc = y - mean
        var = _mm_exact_rhs(yc * yc, gseg) * (1.0 / HEAD_DIM)
        yn = yc * lax.rsqrt(var + RWKV_LN_EPS) * ln_g + ln_b
        gate = p_ref[bi, :, RWKV_SHIFT_COLS:RWKV_COLS]
        o_ref[bi] = ((yn + rows[bi][6]) * _silu(gate)).astype(o_ref.dtype)


def _rwkv(pa, mu, vec, wup, aup, l):
    b, t, _ = pa.shape
    return pl.pallas_call(
        _rwkv_kernel,
        grid=(t // RWKV_TILE,),
        in_specs=[pl.BlockSpec((b, RWKV_TILE, RWKV_COLS), lambda j: (0, j, 0))]
        + [_layer(a, l) for a in (mu, vec, wup, aup)],
        out_specs=pl.BlockSpec((b, RWKV_TILE, GROUP), lambda j: (0, j, 0)),
        out_shape=jax.ShapeDtypeStruct((b, t, GROUP), BF16),
        scratch_shapes=[pltpu.VMEM((b, N_HEADS // PAIR_HEADS, LANES, LANES), F32),
                        pltpu.VMEM((b, 8, RWKV_SHIFT_COLS), F32)],
        compiler_params=pltpu.CompilerParams(dimension_semantics=("arbitrary",), vmem_limit_bytes=VMEM_LIMIT),
        name="rwkv7",
    )(pa, mu, vec, wup, aup)


def _conv_body(p_ref, w_ref, vec_ref, o_ref, u_ref, sh_ref):
    tt = p_ref.shape[0]
    sub = 8
    u_ref[CONV_HALO:CONV_HALO + tt, :] = p_ref[:, 0:GROUP] * _sigmoid(p_ref[:, GROUP:2 * GROUP])
    base = CONV_HALO - (CONV_WIDTH - 1)
    rows = sh_ref.shape[1]
    for b in range(1, sub):
        sh_ref[b - 1] = u_ref[b:b + rows, :]
    acc = jnp.zeros((tt, GROUP), F32)
    for j in range(CONV_WIDTH):
        a, b = divmod(base + j, sub)
        src = u_ref[a * sub:a * sub + tt, :] if b == 0 else sh_ref[b - 1, a * sub:a * sub + tt, :]
        acc = acc + src * w_ref[j:j + 1, :]
    u_ref[0:CONV_HALO, :] = u_ref[tt:tt + CONV_HALO, :]
    y = acc + vec_ref[0:1, :]
    mu = jnp.mean(y, axis=-1, keepdims=True)
    yc = y - mu
    var = jnp.mean(yc * yc, axis=-1, keepdims=True)
    yn = yc * lax.rsqrt(var + LN_EPS) * vec_ref[1:2, :] + vec_ref[2:3, :]
    o_ref[...] = (_silu(yn) * _silu(p_ref[:, 2 * GROUP:3 * GROUP])).astype(o_ref.dtype)


def _hgrn_diag_rows(i, s):
    half = SUB // 2
    per_sub = SUB * half + half * half
    if s < half:
        return i * per_sub + s * SUB, SUB, 0
    return i * per_sub + half * SUB + (s - half) * half, half, half


def _hgrn_body(p_ref, vec_ref, o_ref, s_ref, b_ref, k_ref, v_ref, p_buf, r_buf, fillers=()):
    fillers = list(fillers)

    def fill():
        if fillers:
            fillers.pop(0)()

    c = CHUNK
    nb, tile = p_ref.shape[0], p_ref.shape[1]
    nch = tile // c
    nsub = c // SUB
    sshift = _log2(SUB)
    lb, norm_g = vec_ref[0:1, :], vec_ref[1:2, :]
    gseg = _same_head_ones()
    ltri = _lower_ones(c)
    chains = [(bi, j) for j in range(nch) for bi in range(nb)]
    q, kf, v, bcum = [], [], [], []
    for bi, j in chains:
        rows = slice(j * c, (j + 1) * c)
        q.append(_silu(p_ref[bi, rows, 0:GROUP]))
        f = lb + (1.0 - lb) * _sigmoid(p_ref[bi, rows, GROUP:2 * GROUP])
        kf.append(1.0 - f)
        v.append(p_ref[bi, rows, 2 * GROUP:3 * GROUP])
        bcum.append(jnp.log(f))
    bcum = [_mm_exact_lhs(ltri, lf) for lf in bcum]
    for n in range(len(chains)):
        b_ref[n] = bcum[n]
        k_ref[n] = kf[n]
        v_ref[n] = v[n]
    b_last = [b[c - 1:c, :] for b in bcum]
    same_head = (_iota2((GROUP, GROUP), 0) >> HEAD_SHIFT) == (_iota2((GROUP, GROUP), 1) >> HEAD_SHIFT)
    kv = [jnp.where(same_head, _mm(v[n], kf[n] * jnp.exp(b_last[n] - bcum[n]), TN, HGRN_PASSES), 0.0)
          for n in range(len(chains))]

    trow = _iota2((c, GROUP), 0)
    tsub = trow >> sshift
    q_off, k_cat = [], []
    for n in range(len(chains)):
        beta = b_ref[n, SUB - 1:SUB, :]
        for i in range(2, nsub):
            beta = jnp.where(tsub == i, b_ref[n, i * SUB - 1:i * SUB, :], beta)
        q_off.append(_stack_heads(q[n] * jnp.exp(jnp.where(tsub >= 1, bcum[n] - beta, NEG_BIG))))
        k_cat.append(jnp.concatenate(
            [kf[n] * jnp.exp(jnp.where(trow < i * SUB, b_ref[n, i * SUB - 1:i * SUB, :] - bcum[n], NEG_BIG))
             for i in range(1, nsub)], axis=0))
    att = [_mm(qo, kc, NT, HGRN_PASSES) for qo, kc in zip(q_off, k_cat)]
    arow = (_iota2(att[0].shape, 0) & (c - 1)) >> sshift
    acol = _iota2(att[0].shape, 1) >> _log2(c)
    att = [jnp.where(acol + 1 == arow, a, 0.0) for a in att]
    o_off = [_mm(a, jnp.concatenate([vn] * (nsub - 1), axis=0)) for a, vn in zip(att, v)]
    own = _own_head_lanes(c)
    o_off = [_unstack_heads(jnp.where(own, o, 0.0), c) for o in o_off]

    half = SUB // 2
    for n in range(len(chains)):
        for i in range(nsub):
            for s in range(SUB):
                src = i * SUB + s
                start, nrows, first = _hgrn_diag_rows(i, s)
                tgt = slice(i * SUB + first, (i + 1) * SUB)
                tau = _iota2((nrows, GROUP), 0) + first
                prod = (q[n][tgt] * jnp.exp(jnp.where(tau >= s, bcum[n][tgt] - b_ref[n, src:src + 1, :], NEG_BIG))
                        * k_ref[n, src:src + 1, :])
                p_buf[n, start:start + nrows, :] = prod
        fill()
    for n in range(len(chains)):
        r_buf[n] = _mm_exact_rhs(p_buf[n], gseg, HGRN_DIAG_PIECES)
    o_diag = []
    for n in range(len(chains)):
        fill()
        parts = []
        for i in range(nsub):
            lo = jnp.zeros((half, GROUP), F32)
            hi = jnp.zeros((half, GROUP), F32)
            for s in range(SUB):
                src = i * SUB + s
                start, nrows, first = _hgrn_diag_rows(i, s)
                vs = v_ref[n, src:src + 1, :]
                if first == 0:
                    lo = lo + r_buf[n, start:start + half, :] * vs
                    hi = hi + r_buf[n, start + half:start + SUB, :] * vs
                else:
                    hi = hi + r_buf[n, start:start + half, :] * vs
            parts += [lo, hi]
        o_diag.append(jnp.concatenate(parts, axis=0))

    states = [s_ref[bi] for bi in range(nb)]
    for j in range(nch):
        ns = range(j * nb, (j + 1) * nb)
        o_inter = [_mm(q[n] * jnp.exp(bcum[n]), states[n - j * nb], NT, HGRN_PASSES) for n in ns]
        states = [states[n - j * nb] * jnp.exp(b_last[n]) + kv[n] for n in ns]
        for n, oi in zip(ns, o_inter):
            bi = n - j * nb
            o = oi + o_off[n] + o_diag[n]
            ms = _mm_exact_rhs(o * o, gseg) * (1.0 / HEAD_DIM)
            o = o * lax.rsqrt(ms + RMS_EPS) * norm_g
            gate = p_ref[bi, j * c:(j + 1) * c, 3 * GROUP:4 * GROUP]
            o_ref[bi, j * c:(j + 1) * c, :] = (o * _silu(gate)).astype(o_ref.dtype)
    for bi in range(nb):
        s_ref[bi] = states[bi]
    while fillers:
        fill()


def _mixin_kernel(x_ref, g_ref, wa_ref, wb_ref, wc_ref, wqg_ref, wkv_ref, cw_ref, cvec_ref, hvec_ref,
                  pa_ref, pqg_ref, pkv_ref, yb_ref, yc_ref,
                  pb_s, pc_s, u_ref, sh_ref, s_ref, b_ref, k_ref, v_ref, p_buf, r_buf):
    @pl.when(pl.program_id(1) == 0)
    def _():
        u_ref[0:CONV_HALO, :] = jnp.zeros((CONV_HALO, GROUP), F32)
        s_ref[...] = jnp.zeros_like(s_ref)

    hb = (_rms_scale(x_ref[...]) * g_ref[...]).astype(BF16)
    pb_s[...] = _dg(hb, wb_ref[...])
    pc_s[0] = _dg(hb, wc_ref[...])

    def project(w_ref, o_ref, c0, c1):
        def run():
            o_ref[:, c0:c1] = _dg(hb, w_ref[:, c0:c1]).astype(o_ref.dtype)
        return run

    pieces = [project(w_ref, o_ref, c0, min(c0 + MIXIN_FILL_COLS, w_ref.shape[1]))
              for w_ref, o_ref in ((wa_ref, pa_ref), (wqg_ref, pqg_ref), (wkv_ref, pkv_ref))
              for c0 in range(0, w_ref.shape[1], MIXIN_FILL_COLS)]
    pieces[0]()
    _conv_body(pb_s, cw_ref, cvec_ref, yb_ref, u_ref, sh_ref)
    _hgrn_body(pc_s, hvec_ref, yc_ref, s_ref, b_ref, k_ref, v_ref, p_buf, r_buf, fillers=pieces[1:])


def _mixin(x, g, wa, wb, wc, wqg, wkv, cw, cvec, hvec, l):
    b, t, d = x.shape
    tm = ROW_TILE
    nchains = tm // CHUNK
    diag_rows = (CHUNK // SUB) * (SUB * SUB // 2 + SUB * SUB // 4)
    rowblk = lambda w: pl.BlockSpec((None, tm, w), lambda i, j: (i, j, 0))
    widths = (wa.shape[-1], wqg.shape[-1], wkv.shape[-1])
    return pl.pallas_call(
        _mixin_kernel,
        grid=(b, t // tm),
        in_specs=[rowblk(d)] + [_layer(a, l) for a in (g, wa, wb, wc, wqg, wkv, cw, cvec, hvec)],
        out_specs=[rowblk(w) for w in widths] + [rowblk(GROUP), pl.BlockSpec((1, tm, GROUP), lambda i, j: (i, j, 0))],
        out_shape=[jax.ShapeDtypeStruct((b, t, w), dt) for w, dt in zip(widths, (F32, F32, BF16))]
        + [jax.ShapeDtypeStruct((b, t, GROUP), BF16)] * 2,
        scratch_shapes=[pltpu.VMEM((tm, wb.shape[-1]), F32), pltpu.VMEM((1, tm, wc.shape[-1]), F32),
                        pltpu.VMEM((CONV_HALO + tm, GROUP), F32), pltpu.VMEM((7, CONV_HALO + tm - 8, GROUP), F32),
                        pltpu.VMEM((1, GROUP, GROUP), F32)] + [pltpu.VMEM((nchains, CHUNK, GROUP), F32)] * 3
        + [pltpu.VMEM((nchains, diag_rows, GROUP), F32)] * 2,
        compiler_params=pltpu.CompilerParams(dimension_semantics=("arbitrary", "arbitrary"),
                                             vmem_limit_bytes=VMEM_LIMIT),
        name="inproj_conv_hgrn",
    )(x, g, wa, wb, wc, wqg, wkv, cw, cvec, hvec)


def _sb_kernel(q_ref, k_ref, v_ref, gate_ref, o_ref, acc_ref, carry_ref, vst_ref):
    tq = SB_BLOCK
    nq = SB_QBLOCKS
    n = N_HEADS * tq
    first_qb = pl.program_id(1) * nq

    @pl.when(pl.program_id(1) == 0)
    def _():
        def fill(j, c):
            vst_ref[j] = _stack_heads(v_ref[pl.ds(pl.multiple_of(j * tq, tq), tq), :])
            return c
        lax.fori_loop(0, v_ref.shape[0] // tq, fill, 0)

    qs = [_stack_heads(q_ref[i * tq:(i + 1) * tq, :] * (HEAD_DIM ** -0.5 * LOG2E)).astype(BF16) for i in range(nq)]
    acc_ref[...] = jnp.zeros_like(acc_ref)
    carry_ref[...] = jnp.zeros_like(carry_ref)

    causal_gap = _iota2((n, tq), 1) - (_iota2((n, tq), 0) & (tq - 1))
    cs_mat = jnp.where(_iota2((tq, tq), 0) > _iota2((tq, tq), 1), 1.0, 0.0).astype(BF16)

    def body(state):
        it, _ = state
        js = [first_qb + i - it for i in range(nq)]
        jcs = [jnp.maximum(j, 0) for j in js]
        zs = [_dg(qs[i], k_ref[pl.ds(pl.multiple_of(jcs[i] * tq, tq), tq), :], NT) for i in range(nq)]
        log_keeps = [-(jnp.maximum(z, 0.0) + jnp.log2(1.0 + jnp.exp2(-jnp.abs(z)))) for z in zs]
        masks = [causal_gap < jnp.where(j >= 0, it * tq, -tq) for j in js]
        lkms = [jnp.where(m, lk, 0.0) for m, lk in zip(masks, log_keeps)]
        css = [_mm_exact_rhs(lkm, cs_mat, SB_CUMSUM_PIECES) for lkm in lkms]
        atts = []
        more = jnp.bool_(False)
        for i in range(nq):
            carry = carry_ref[i]
            att = jnp.where(masks[i], jnp.exp2(zs[i] + log_keeps[i] + css[i] + carry), 0.0).astype(BF16)
            atts.append(jnp.concatenate([att[h * tq:(h + 1) * tq] for h in range(N_HEADS)], axis=1))
            carry = carry + jnp.sum(lkms[i], axis=1, keepdims=True)
            carry_ref[i] = carry
            more = more | ((js[i] > 0) & (jnp.max(carry) > SB_LOG_UNDERFLOW * LOG2E))
        avs = [_dg(atts[i], vst_ref[jcs[i]]) for i in range(nq)]
        for i in range(nq):
            acc_ref[i] += avs[i]
        return it + 1, more.astype(jnp.int32)

    lax.while_loop(lambda s: s[1] > 0, body, (jnp.int32(0), jnp.int32(1)))
    for i in range(nq):
        o_ref[i * tq:(i + 1) * tq, :] = (acc_ref[i] * _silu(gate_ref[i * tq:(i + 1) * tq, :])).astype(o_ref.dtype)


def _sb(pqg, pkv):
    b, t, _ = pqg.shape
    tq = SB_BLOCK
    nq = SB_QBLOCKS
    blk = lambda cidx: pl.BlockSpec((None, nq * tq, GROUP), lambda i, j: (i, j, cidx))
    full = lambda cidx: pl.BlockSpec((None, t, GROUP), lambda i, j: (i, 0, cidx))
    return pl.pallas_call(
        _sb_kernel,
        grid=(b, t // (nq * tq)),
        in_specs=[blk(0), full(0), full(1), blk(1)],
        out_specs=pl.BlockSpec((None, nq * tq, GROUP), lambda i, j: (i, j, 0)),
        out_shape=jax.ShapeDtypeStruct((b, t, GROUP), BF16),
        scratch_shapes=[pltpu.VMEM((nq, tq, GROUP), F32), pltpu.VMEM((nq, N_HEADS * tq, tq), F32),
                        pltpu.VMEM((t // tq, N_HEADS * tq, GROUP), BF16)],
        compiler_params=pltpu.CompilerParams(dimension_semantics=("arbitrary", "arbitrary"),
                                             vmem_limit_bytes=VMEM_LIMIT),
        name="stick_breaking",
    )(pqg, pkv, pkv, pqg)


def _memkv_kernel(m_ref, g_ref, wk_ref, wv_ref, k_ref, v_ref):
    h = (_rms_scale(m_ref[...]) * g_ref[...]).astype(BF16)
    k_ref[...] = _dg(h, wk_ref[...]).astype(k_ref.dtype)
    v_ref[...] = _dg(h, wv_ref[...]).astype(v_ref.dtype)


def _memkv(mem2d, g, wk, wv, l):
    n, d = mem2d.shape
    tm = ROW_TILE
    row = lambda i: (i, 0)
    return pl.pallas_call(
        _memkv_kernel,
        grid=(n // tm,),
        in_specs=[pl.BlockSpec((tm, d), row), _layer(g, l), _layer(wk, l), _layer(wv, l)],
        out_specs=[pl.BlockSpec((tm, d), row)] * 2,
        out_shape=[jax.ShapeDtypeStruct((n, d), BF16)] * 2,
        compiler_params=pltpu.CompilerParams(dimension_semantics=("arbitrary",), vmem_limit_bytes=VMEM_LIMIT),
        name="mem_kv",
    )(mem2d, g, wk, wv)


def _post_kernel(x_ref, ya_ref, yb_ref, yc_ref, yd_ref, wout_ref, gx_ref, wq_ref, k_ref, v_ref, wo_ref,
                 gf_ref, o_ref, *, final):
    d = x_ref.shape[-1]
    hd = d // XATTN_HEADS
    x1 = x_ref[...]
    for g, y_ref in enumerate((ya_ref, yb_ref, yc_ref, yd_ref)):
        x1 = x1 + _dg(y_ref[...], wout_ref[g * GROUP:(g + 1) * GROUP, :])
    hx = (_rms_scale(x1) * gx_ref[...]).astype(BF16)
    q = _dg(hx, wq_ref[...])
    outs = []
    for h in range(XATTN_HEADS):
        sl = slice(h * hd, (h + 1) * hd)
        s = _dg(q[:, sl].astype(BF16), k_ref[:, sl], NT) * (hd ** -0.5)
        s = s - jnp.max(s, axis=-1, keepdims=True)
        e = jnp.exp(s)
        p = e / jnp.sum(e, axis=-1, keepdims=True)
        outs.append(_dg(p.astype(BF16), v_ref[:, sl]).astype(BF16))
    x2 = x1 + _dg(jnp.concatenate(outs, axis=-1), wo_ref[...])
    if final:
        x2 = _rms_scale(x2) * gf_ref[...]
    o_ref[...] = x2


def _post(x, ys, wout, gx, wq, kmem, vmem, wo, gf, l, final):
    b, t, d = x.shape
    m = kmem.shape[1]
    tm = POST_ROW_TILE
    rowblk = lambda w: pl.BlockSpec((None, tm, w), lambda i, j: (i, j, 0))
    return pl.pallas_call(
        functools.partial(_post_kernel, final=final),
        grid=(b, t // tm),
        in_specs=[rowblk(d)] + [rowblk(GROUP)] * 4
        + [_layer(wout, l), _layer(gx, l), _layer(wq, l),
           pl.BlockSpec((None, m, d), lambda i, j: (i, 0, 0)), pl.BlockSpec((None, m, d), lambda i, j: (i, 0, 0)),
           _layer(wo, l), pl.BlockSpec((1, d), lambda i, j: (0, 0))],
        out_specs=rowblk(d),
        out_shape=jax.ShapeDtypeStruct((b, t, d), F32),
        compiler_params=pltpu.CompilerParams(dimension_semantics=("arbitrary", "arbitrary"),
                                             vmem_limit_bytes=VMEM_LIMIT),
        name="outproj_xattn",
    )(x, *ys, wout, gx, wq, kmem, vmem, wo, gf)


def _rows8(*rows):
    rows = [r.reshape(r.shape[0], 1, -1).astype(F32) for r in rows]
    depth, _, width = rows[0].shape
    return jnp.concatenate(rows + [jnp.zeros((depth, 8 - len(rows), width), F32)], axis=1)


def kernel(x, mem, norm_mix, w_in, rwkv_mu, rwkv_w0, rwkv_w_up, rwkv_a0, rwkv_a_up, rwkv_k_k, rwkv_k_a,
           rwkv_r_k, rwkv_ln_g, rwkv_ln_b, conv_w, conv_b, conv_ln_g, conv_ln_b, hgrn_lb_logits, hgrn_norm_g,
           w_out, norm_xattn, norm_mem, xattn_wq, xattn_wk, xattn_wv, xattn_wo, norm_final):
    b, t, d = x.shape
    depth = w_in.shape[0]
    n_shift = 3 * GROUP + 2 * RWKV_LORA
    n_rwkv = n_shift + GROUP
    lb_soft = jax.nn.softmax(hgrn_lb_logits.astype(F32), axis=0)
    lower_bounds = jnp.cumsum(lb_soft, axis=0) - lb_soft[0]
    mem2d = mem.reshape(-1, d)
    pad_lora = RWKV_SHIFT_COLS - n_shift
    lora_rows = LANES - 2 * RWKV_LORA

    zeros = lambda *shape: jnp.zeros((depth,) + shape, F32)
    wa = jnp.concatenate([w_in[:, :, :n_shift], zeros(d, pad_lora), w_in[:, :, n_shift:n_rwkv]], axis=2).astype(BF16)
    wb = w_in[:, :, n_rwkv:n_rwkv + 3 * GROUP].astype(BF16)
    wc = w_in[:, :, n_rwkv + 3 * GROUP:n_rwkv + 7 * GROUP].astype(BF16)
    sb0 = n_rwkv + 7 * GROUP
    wqg = jnp.concatenate([w_in[:, :, sb0:sb0 + GROUP], w_in[:, :, sb0 + 3 * GROUP:]], axis=2).astype(BF16)
    wkv = w_in[:, :, sb0 + GROUP:sb0 + 3 * GROUP].astype(BF16)
    g_mix = norm_mix.reshape(depth, 1, d)
    mu = jnp.concatenate([rwkv_mu, zeros(pad_lora)], axis=1).reshape(depth, 1, -1)
    rvec = _rows8(rwkv_w0, rwkv_a0, rwkv_k_k, rwkv_k_a, rwkv_r_k, rwkv_ln_g, rwkv_ln_b)
    wup = jnp.concatenate([rwkv_w_up, zeros(RWKV_LORA + lora_rows, GROUP)], axis=1)
    aup = jnp.concatenate([zeros(RWKV_LORA, GROUP), rwkv_a_up, zeros(lora_rows, GROUP)], axis=1)
    cw = jnp.concatenate([conv_w, zeros(32 - CONV_WIDTH, GROUP)], axis=1)
    cvec = _rows8(conv_b, conv_ln_g, conv_ln_b)
    hvec = _rows8(lower_bounds, hgrn_norm_g)
    g_mem = norm_mem.reshape(depth, 1, d)
    g_x = norm_xattn.reshape(depth, 1, d)
    wk, wv, wq, wo, wout = (a.astype(BF16) for a in (xattn_wk, xattn_wv, xattn_wq, xattn_wo, w_out))

    for l in range(depth):
        pa, pqg, pkv, y_b, y_c = _mixin(x, g_mix, wa, wb, wc, wqg, wkv, cw, cvec, hvec, l)
        y_a = _rwkv(pa, mu, rvec, wup, aup, l)
        y_d = _sb(pqg, pkv)
        kmem, vmem = _memkv(mem2d, g_mem, wk, wv, l)
        x = _post(x, (y_a, y_b, y_c, y_d), wout, g_x, wq, kmem.reshape(b, -1, d), vmem.reshape(b, -1, d), wo,
                  norm_final.reshape(1, d), l, final=(l == depth - 1))
    return x
```

```python
import functools

import jax
import jax.numpy as jnp
from jax import lax
from jax.experimental import pallas as pl
from jax.experimental.pallas import tpu as pltpu

F32 = jnp.float32
BF16 = jnp.bfloat16

GROUP = 256
N_HEADS = 4
HEAD_DIM = GROUP // N_HEADS
RWKV_LORA = 32
CONV_WIDTH = 31
XATTN_HEADS = 4
RMS_EPS = 1e-6
LN_EPS = 1e-5
RWKV_LN_EPS = 64e-5

LANES = 128
RWKV_SHIFT_COLS = 3 * GROUP + LANES
RWKV_COLS = RWKV_SHIFT_COLS + GROUP

CHUNK = 64
RWKV_TILE = 8 * CHUNK
SUB = 16
HGRN_PASSES = 1
HGRN_DIAG_PIECES = 1
SB_BLOCK = 128
SB_QBLOCKS = 4
SB_CUMSUM_PIECES = 2
CONV_HALO = 32
ROW_TILE = 512
MIXIN_FILL_COLS = 256
POST_ROW_TILE = 1024
SB_LOG_UNDERFLOW = -120.0
LOG2E = 1.4426950408889634
NEG_BIG = -1e30
VMEM_LIMIT = 56 * 1024 * 1024

NN = ((1,), (0,))
NT = ((1,), (1,))
TN = ((0,), (0,))


def _pieces(x, n):
    out = []
    for _ in range(n - 1):
        p = x.astype(BF16)
        out.append(p)
        x = x - p.astype(F32)
    out.append(x.astype(BF16))
    return out


def _dg(a, b, dims=NN):
    return lax.dot_general(a, b, (dims, ((), ())), preferred_element_type=F32)


def _mm(a, b, dims=NN, passes=1):
    if passes == 1:
        return _dg(a.astype(BF16), b.astype(BF16), dims)
    a_hi, a_lo = _pieces(a, 2)
    b_hi, b_lo = _pieces(b, 2)
    return _dg(jnp.concatenate([a_hi, a_hi, a_lo], axis=dims[0][0]),
               jnp.concatenate([b_hi, b_lo, b_hi], axis=dims[1][0]), dims)


SUM_PIECES = 2


def _mm_exact_rhs(x, w_bf16, n=SUM_PIECES):
    return _dg(jnp.concatenate(_pieces(x, n), axis=1), jnp.concatenate([w_bf16] * n, axis=0))


def _mm_exact_lhs(w_bf16, x, n=SUM_PIECES):
    return _dg(jnp.concatenate([w_bf16] * n, axis=1), jnp.concatenate(_pieces(x, n), axis=0))


def _iota2(shape, dim):
    return lax.broadcasted_iota(jnp.int32, shape, dim)


def _log2(n):
    assert n & (n - 1) == 0
    return n.bit_length() - 1


HEAD_SHIFT = _log2(HEAD_DIM)


def _same_head_ones(n=GROUP):
    r = _iota2((n, n), 0) >> HEAD_SHIFT
    c = _iota2((n, n), 1) >> HEAD_SHIFT
    return jnp.where(r == c, 1.0, 0.0).astype(BF16)


def _lower_ones(n):
    return jnp.where(_iota2((n, n), 1) <= _iota2((n, n), 0), 1.0, 0.0).astype(BF16)


def _stack_heads(x):
    head = _iota2(x.shape, 1) >> HEAD_SHIFT
    return jnp.concatenate([jnp.where(head == h, x, 0.0) for h in range(N_HEADS)], axis=0)


PAIR_HEADS = LANES // HEAD_DIM


def _stack_pair(x, g):
    xg = x[:, g * LANES:(g + 1) * LANES]
    head = _iota2(xg.shape, 1) >> HEAD_SHIFT
    return jnp.concatenate([jnp.where(head == h, xg, 0.0) for h in range(PAIR_HEADS)], axis=0)


def _unstack_heads(y, t):
    return functools.reduce(lambda s, u: s + u, [y[h * t:(h + 1) * t] for h in range(N_HEADS)])


def _own_head_lanes(t):
    return (_iota2((N_HEADS * t, GROUP), 1) >> HEAD_SHIFT) == (_iota2((N_HEADS * t, GROUP), 0) >> _log2(t))


def _sigmoid(x):
    return 1.0 / (1.0 + jnp.exp(-x))


def _silu(x):
    return x * _sigmoid(x)


def _softplus(x):
    return jnp.maximum(x, 0.0) + jnp.log(1.0 + jnp.exp(-jnp.abs(x)))


def _rms_scale(x):
    return x * lax.rsqrt(jnp.mean(x * x, axis=-1, keepdims=True) + RMS_EPS)


def _layer(arr, l):
    shape = arr.shape[1:]
    return pl.BlockSpec((None,) + shape, lambda *_: (l,) + (0,) * len(shape))


def _rwkv_rows(p, prev_row, mu, vec, wup, aup):
    row = _iota2(p.shape, 0)
    p_prev = jnp.where(row == 0, prev_row, pltpu.roll(p, 1, 0))
    m = p + (p_prev - p) * mu
    r = m[:, 0:GROUP]
    k = m[:, GROUP:2 * GROUP]
    v = m[:, 2 * GROUP:3 * GROUP]
    lora = m[:, 3 * GROUP:RWKV_SHIFT_COLS]
    w0, a0, k_k, k_a, r_k = vec[0:1, :], vec[1:2, :], vec[2:3, :], vec[3:4, :], vec[4:5, :]
    gseg = _same_head_ones()
    w_log = -_softplus(-(w0 + _mm(jnp.tanh(lora), wup))) - 0.5
    lw = -jnp.exp(w_log)
    a_sig = _sigmoid(a0 + _mm(lora, aup))
    kk = k * k_k
    kk = kk / jnp.maximum(jnp.sqrt(_mm_exact_rhs(kk * kk, gseg)), 1e-12)
    k2 = k * (1.0 + (a_sig - 1.0) * k_a)
    bonus = _mm_exact_rhs(r * k2 * r_k, gseg) * v
    return r, k2, v, kk, a_sig, lw, bonus


def _rwkv_prep(chains):
    c = CHUNK
    n = PAIR_HEADS * c
    ltri = _lower_ones(c)
    ri = _iota2((n, n), 0)
    ci = _iota2((n, n), 1)
    cshift = _log2(c)
    same = (ri >> cshift) == (ci >> cshift)
    strict = same & (ci < ri)
    incl = same & (ci <= ri)
    fs = []
    for r, k2, v, kk, a_sig, lw in chains:
        cum = _mm_exact_lhs(ltri, lw)
        cum_last = cum[c - 1:c, :]
        g_inv = jnp.exp(-cum)
        g_rem = jnp.exp(cum_last - cum)
        a_dec = -kk * jnp.exp(cum - lw)
        r_dec = r * jnp.exp(cum)
        b = kk * a_sig
        decay = jnp.exp(cum_last)
        for g in range(N_HEADS // PAIR_HEADS):
            fs.append(dict(a_t=_stack_pair(a_dec, g), r_t=_stack_pair(r_dec, g),
                           b_t=_stack_pair(b * g_inv, g), k_t=_stack_pair(k2 * g_inv, g),
                           b_h=_stack_pair(b * g_rem, g), k_h=_stack_pair(k2 * g_rem, g),
                           v_s=_stack_pair(v, g), decay=decay[:, g * LANES:(g + 1) * LANES]))
    pairs = [_dg(jnp.concatenate([f["a_t"], f["r_t"]], axis=0).astype(BF16),
                 jnp.concatenate([f["b_t"], f["k_t"]], axis=0).astype(BF16), NT) for f in fs]
    for f, m in zip(fs, pairs):
        f["a_ab"] = jnp.where(strict, m[0:n, 0:n], 0.0)
        f["a_ak"] = jnp.where(strict, m[0:n, n:2 * n], 0.0)
        f["a_rb"] = jnp.where(incl, m[n:2 * n, 0:n], 0.0)
        f["a_rk"] = jnp.where(incl, m[n:2 * n, n:2 * n], 0.0)

    eye = jnp.where(ri == ci, 1.0, 0.0)
    for f in fs:
        f["tinv"] = eye + jnp.where((ri >> 1) == (ci >> 1), f["a_ab"], 0.0)
    for s in range(2, cshift + 1):
        lower_left = ((ri >> s) == (ci >> s)) & ((ri >> (s - 1)) != (ci >> (s - 1)))
        xs = [_mm(jnp.where(lower_left, f["a_ab"], 0.0), f["tinv"]) for f in fs]
        xs = [_mm(f["tinv"], x) for f, x in zip(fs, xs)]
        for f, x in zip(fs, xs):
            f["tinv"] = f["tinv"] + x
    for f in fs:
        f["w_m"] = _mm(f["tinv"], f["a_t"])
    xs = [_mm(f["a_ak"], f["v_s"]) for f in fs]
    for f, x in zip(fs, xs):
        f["z_m"] = _mm(f["tinv"], x)
    for f in fs:
        f["a_r"] = jnp.concatenate([f["a_rb"], f["a_rk"]], axis=1).astype(BF16)
        f["bk_h"] = jnp.concatenate([f["b_h"], f["k_h"]], axis=0).astype(BF16)
        f["wr"] = jnp.concatenate([f["w_m"], f["r_t"]], axis=0).astype(BF16)
        f["v_s"] = f["v_s"].astype(BF16)
    return fs


def _rwkv_step(fs, states):
    n = PAIR_HEADS * CHUNK
    sb = [s.astype(BF16) for s in states]
    wrs = [_dg(f["wr"], s, NT) for f, s in zip(fs, sb)]
    uvs = [jnp.concatenate([(wr[0:n] + f["z_m"]).astype(BF16), f["v_s"]], axis=0) for f, wr in zip(fs, wrs)]
    ys = [wr[n:2 * n] + _dg(f["a_r"], uv) for f, wr, uv in zip(fs, wrs, uvs)]
    new = [s * f["decay"] + _dg(uv, f["bk_h"], TN) for f, uv, s in zip(fs, uvs, states)]
    return ys, new


def _rwkv_kernel(p_ref, mu_ref, vec_ref, wup_ref, aup_ref, o_ref, s_ref, prev_ref):
    @pl.when(pl.program_id(0) == 0)
    def _():
        s_ref[...] = jnp.zeros_like(s_ref)
        prev_ref[...] = jnp.zeros_like(prev_ref)

    c = CHUNK
    nb, tile = p_ref.shape[0], p_ref.shape[1]
    vec = vec_ref[...]
    ln_g, ln_b = vec[5:6, :], vec[6:7, :]
    gseg = _same_head_ones()
    nch = tile // c
    rows = []
    for bi in range(nb):
        p = p_ref[bi, :, 0:RWKV_SHIFT_COLS]
        rows.append(_rwkv_rows(p, prev_ref[bi, 0:1, :], mu_ref[...], vec, wup_ref[...], aup_ref[...]))
        prev_ref[bi, 0:1, :] = p[tile - 1:tile, :]
    npair = N_HEADS // PAIR_HEADS
    factors = _rwkv_prep([tuple(a[j * c:(j + 1) * c] for a in rows[bi][:6]) for j in range(nch) for bi in range(nb)])
    states = [s_ref[bi, g] for bi in range(nb) for g in range(npair)]
    y_parts = [[] for _ in range(nb)]
    for j in range(nch):
        ys, states = _rwkv_step(factors[j * nb * npair:(j + 1) * nb * npair], states)
        for bi in range(nb):
            pair_ys = ys[bi * npair:(bi + 1) * npair]
            y_parts[bi].append(jnp.concatenate(
                [functools.reduce(lambda s, t: s + t, [y[h * c:(h + 1) * c] for h in range(PAIR_HEADS)])
                 for y in pair_ys], axis=1))
    for bi in range(nb):
        for g in range(npair):
            s_ref[bi, g] = states[bi * npair + g]
        y = jnp.concatenate(y_parts[bi], axis=0)
        mean = _mm_exact_rhs(y, gseg) * (1.0 / HEAD_DIM)
        yc = y - mean
        var = _mm_exact_rhs(yc * yc, gseg) * (1.0 / HEAD_DIM)
        yn = yc * lax.rsqrt(var + RWKV_LN_EPS) * ln_g + ln_b
        gate = p_ref[bi, :, RWKV_SHIFT_COLS:RWKV_COLS]
        o_ref[bi] = ((yn + rows[bi][6]) * _silu(gate)).astype(o_ref.dtype)


def _rwkv(pa, mu, vec, wup, aup, l):
    b, t, _ = pa.shape
    return pl.pallas_call(
        _rwkv_kernel,
        grid=(t // RWKV_TILE,),
        in_specs=[pl.BlockSpec((b, RWKV_TILE, RWKV_COLS), lambda j: (0, j, 0))]
        + [_layer(a, l) for a in (mu, vec, wup, aup)],
        out_specs=pl.BlockSpec((b, RWKV_TILE, GROUP), lambda j: (0, j, 0)),
        out_shape=jax.ShapeDtypeStruct((b, t, GROUP), BF16),
        scratch_shapes=[pltpu.VMEM((b, N_HEADS // PAIR_HEADS, LANES, LANES), F32),
                        pltpu.VMEM((b, 8, RWKV_SHIFT_COLS), F32)],
        compiler_params=pltpu.CompilerParams(dimension_semantics=("arbitrary",), vmem_limit_bytes=VMEM_LIMIT),
        name="rwkv7",
    )(pa, mu, vec, wup, aup)


def _conv_body(p_ref, w_ref, vec_ref, o_ref, u_ref, sh_ref):
    tt = p_ref.shape[0]
    sub = 8
    u_ref[CONV_HALO:CONV_HALO + tt, :] = p_ref[:, 0:GROUP] * _sigmoid(p_ref[:, GROUP:2 * GROUP])
    base = CONV_HALO - (CONV_WIDTH - 1)
    rows = sh_ref.shape[1]
    for b in range(1, sub):
        sh_ref[b - 1] = u_ref[b:b + rows, :]
    acc = jnp.zeros((tt, GROUP), F32)
    for j in range(CONV_WIDTH):
        a, b = divmod(base + j, sub)
        src = u_ref[a * sub:a * sub + tt, :] if b == 0 else sh_ref[b - 1, a * sub:a * sub + tt, :]
        acc = acc + src * w_ref[j:j + 1, :]
    u_ref[0:CONV_HALO, :] = u_ref[tt:tt + CONV_HALO, :]
    y = acc + vec_ref[0:1, :]
    mu = jnp.mean(y, axis=-1, keepdims=True)
    yc = y - mu
    var = jnp.mean(yc * yc, axis=-1, keepdims=True)
    yn = yc * lax.rsqrt(var + LN_EPS) * vec_ref[1:2, :] + vec_ref[2:3, :]
    o_ref[...] = (_silu(yn) * _silu(p_ref[:, 2 * GROUP:3 * GROUP])).astype(o_ref.dtype)


def _hgrn_diag_rows(i, s):
    half = SUB // 2
    per_sub = SUB * half + half * half
    if s < half:
        return i * per_sub + s * SUB, SUB, 0
    return i * per_sub + half * SUB + (s - half) * half, half, half


def _hgrn_body(p_ref, vec_ref, o_ref, s_ref, b_ref, k_ref, v_ref, p_buf, r_buf, fillers=()):
    fillers = list(fillers)

    def fill():
        if fillers:
            fillers.pop(0)()

    c = CHUNK
    nb, tile = p_ref.shape[0], p_ref.shape[1]
    nch = tile // c
    nsub = c // SUB
    sshift = _log2(SUB)
    lb, norm_g = vec_ref[0:1, :], vec_ref[1:2, :]
    gseg = _same_head_ones()
    ltri = _lower_ones(c)
    chains = [(bi, j) for j in range(nch) for bi in range(nb)]
    q, kf, v, bcum = [], [], [], []
    for bi, j in chains:
        rows = slice(j * c, (j + 1) * c)
        q.append(_silu(p_ref[bi, rows, 0:GROUP]))
        f = lb + (1.0 - lb) * _sigmoid(p_ref[bi, rows, GROUP:2 * GROUP])
        kf.append(1.0 - f)
        v.append(p_ref[bi, rows, 2 * GROUP:3 * GROUP])
        bcum.append(jnp.log(f))
    bcum = [_mm_exact_lhs(ltri, lf) for lf in bcum]
    for n in range(len(chains)):
        b_ref[n] = bcum[n]
        k_ref[n] = kf[n]
        v_ref[n] = v[n]
    b_last = [b[c - 1:c, :] for b in bcum]
    same_head = (_iota2((GROUP, GROUP), 0) >> HEAD_SHIFT) == (_iota2((GROUP, GROUP), 1) >> HEAD_SHIFT)
    kv = [jnp.where(same_head, _mm(v[n], kf[n] * jnp.exp(b_last[n] - bcum[n]), TN, HGRN_PASSES), 0.0)
          for n in range(len(chains))]

    trow = _iota2((c, GROUP), 0)
    tsub = trow >> sshift
    q_off, k_cat = [], []
    for n in range(len(chains)):
        beta = b_ref[n, SUB - 1:SUB, :]
        for i in range(2, nsub):
            beta = jnp.where(tsub == i, b_ref[n, i * SUB - 1:i * SUB, :], beta)
        q_off.append(_stack_heads(q[n] * jnp.exp(jnp.where(tsub >= 1, bcum[n] - beta, NEG_BIG))))
        k_cat.append(jnp.concatenate(
            [kf[n] * jnp.exp(jnp.where(trow < i * SUB, b_ref[n, i * SUB - 1:i * SUB, :] - bcum[n], NEG_BIG))
             for i in range(1, nsub)], axis=0))
    att = [_mm(qo, kc, NT, HGRN_PASSES) for qo, kc in zip(q_off, k_cat)]
    arow = (_iota2(att[0].shape, 0) & (c - 1)) >> sshift
    acol = _iota2(att[0].shape, 1) >> _log2(c)
    att = [jnp.where(acol + 1 == arow, a, 0.0) for a in att]
    o_off = [_mm(a, jnp.concatenate([vn] * (nsub - 1), axis=0)) for a, vn in zip(att, v)]
    own = _own_head_lanes(c)
    o_off = [_unstack_heads(jnp.where(own, o, 0.0), c) for o in o_off]

    half = SUB // 2
    for n in range(len(chains)):
        for i in range(nsub):
            for s in range(SUB):
                src = i * SUB + s
                start, nrows, first = _hgrn_diag_rows(i, s)
                tgt = slice(i * SUB + first, (i + 1) * SUB)
                tau = _iota2((nrows, GROUP), 0) + first
                prod = (q[n][tgt] * jnp.exp(jnp.where(tau >= s, bcum[n][tgt] - b_ref[n, src:src + 1, :], NEG_BIG))
                        * k_ref[n, src:src + 1, :])
                p_buf[n, start:start + nrows, :] = prod
        fill()
    for n in range(len(chains)):
        r_buf[n] = _mm_exact_rhs(p_buf[n], gseg, HGRN_DIAG_PIECES)
    o_diag = []
    for n in range(len(chains)):
        fill()
        parts = []
        for i in range(nsub):
            lo = jnp.zeros((half, GROUP), F32)
            hi = jnp.zeros((half, GROUP), F32)
            for s in range(SUB):
                src = i * SUB + s
                start, nrows, first = _hgrn_diag_rows(i, s)
                vs = v_ref[n, src:src + 1, :]
                if first == 0:
                    lo = lo + r_buf[n, start:start + half, :] * vs
                    hi = hi + r_buf[n, start + half:start + SUB, :] * vs
                else:
                    hi = hi + r_buf[n, start:start + half, :] * vs
            parts += [lo, hi]
        o_diag.append(jnp.concatenate(parts, axis=0))

    states = [s_ref[bi] for bi in range(nb)]
    for j in range(nch):
        ns = range(j * nb, (j + 1) * nb)
        o_inter = [_mm(q[n] * jnp.exp(bcum[n]), states[n - j * nb], NT, HGRN_PASSES) for n in ns]
        states = [states[n - j * nb] * jnp.exp(b_last[n]) + kv[n] for n in ns]
        for n, oi in zip(ns, o_inter):
            bi = n - j * nb
            o = oi + o_off[n] + o_diag[n]
            ms = _mm_exact_rhs(o * o, gseg) * (1.0 / HEAD_DIM)
            o = o * lax.rsqrt(ms + RMS_EPS) * norm_g
            gate = p_ref[bi, j * c:(j + 1) * c, 3 * GROUP:4 * GROUP]
            o_ref[bi, j * c:(j + 1) * c, :] = (o * _silu(gate)).astype(o_ref.dtype)
    for bi in range(nb):
        s_ref[bi] = states[bi]
    while fillers:
        fill()


def _mixin_kernel(x_ref, g_ref, wa_ref, wb_ref, wc_ref, wqg_ref, wkv_ref, cw_ref, cvec_ref, hvec_ref,
                  pa_ref, pqg_ref, pkv_ref, yb_ref, yc_ref,
                  pb_s, pc_s, u_ref, sh_ref, s_ref, b_ref, k_ref, v_ref, p_buf, r_buf):
    @pl.when(pl.program_id(1) == 0)
    def _():
        u_ref[0:CONV_HALO, :] = jnp.zeros((CONV_HALO, GROUP), F32)
        s_ref[...] = jnp.zeros_like(s_ref)

    hb = (_rms_scale(x_ref[...]) * g_ref[...]).astype(BF16)
    pb_s[...] = _dg(hb, wb_ref[...])
    pc_s[0] = _dg(hb, wc_ref[...])

    def project(w_ref, o_ref, c0, c1):
        def run():
            o_ref[:, c0:c1] = _dg(hb, w_ref[:, c0:c1]).astype(o_ref.dtype)
        return run

    pieces = [project(w_ref, o_ref, c0, min(c0 + MIXIN_FILL_COLS, w_ref.shape[1]))
              for w_ref, o_ref in ((wa_ref, pa_ref), (wqg_ref, pqg_ref), (wkv_ref, pkv_ref))
              for c0 in range(0, w_ref.shape[1], MIXIN_FILL_COLS)]
    pieces[0]()
    _conv_body(pb_s, cw_ref, cvec_ref, yb_ref, u_ref, sh_ref)
    _hgrn_body(pc_s, hvec_ref, yc_ref, s_ref, b_ref, k_ref, v_ref, p_buf, r_buf, fillers=pieces[1:])


def _mixin(x, g, wa, wb, wc, wqg, wkv, cw, cvec, hvec, l):
    b, t, d = x.shape
    tm = ROW_TILE
    nchains = tm // CHUNK
    diag_rows = (CHUNK // SUB) * (SUB * SUB // 2 + SUB * SUB // 4)
    rowblk = lambda w: pl.BlockSpec((None, tm, w), lambda i, j: (i, j, 0))
    widths = (wa.shape[-1], wqg.shape[-1], wkv.shape[-1])
    return pl.pallas_call(
        _mixin_kernel,
        grid=(b, t // tm),
        in_specs=[rowblk(d)] + [_layer(a, l) for a in (g, wa, wb, wc, wqg, wkv, cw, cvec, hvec)],
        out_specs=[rowblk(w) for w in widths] + [rowblk(GROUP), pl.BlockSpec((1, tm, GROUP), lambda i, j: (i, j, 0))],
        out_shape=[jax.ShapeDtypeStruct((b, t, w), dt) for w, dt in zip(widths, (F32, F32, BF16))]
        + [jax.ShapeDtypeStruct((b, t, GROUP), BF16)] * 2,
        scratch_shapes=[pltpu.VMEM((tm, wb.shape[-1]), F32), pltpu.VMEM((1, tm, wc.shape[-1]), F32),
                        pltpu.VMEM((CONV_HALO + tm, GROUP), F32), pltpu.VMEM((7, CONV_HALO + tm - 8, GROUP), F32),
                        pltpu.VMEM((1, GROUP, GROUP), F32)] + [pltpu.VMEM((nchains, CHUNK, GROUP), F32)] * 3
        + [pltpu.VMEM((nchains, diag_rows, GROUP), F32)] * 2,
        compiler_params=pltpu.CompilerParams(dimension_semantics=("arbitrary", "arbitrary"),
                                             vmem_limit_bytes=VMEM_LIMIT),
        name="inproj_conv_hgrn",
    )(x, g, wa, wb, wc, wqg, wkv, cw, cvec, hvec)


def _sb_kernel(q_ref, k_ref, v_ref, gate_ref, o_ref, acc_ref, carry_ref, vst_ref):
    tq = SB_BLOCK
    nq = SB_QBLOCKS
    n = N_HEADS * tq
    first_qb = pl.program_id(1) * nq

    @pl.when(pl.program_id(1) == 0)
    def _():
        def fill(j, c):
            vst_ref[j] = _stack_heads(v_ref[pl.ds(pl.multiple_of(j * tq, tq), tq), :])
            return c
        lax.fori_loop(0, v_ref.shape[0] // tq, fill, 0)

    qs = [_stack_heads(q_ref[i * tq:(i + 1) * tq, :] * (HEAD_DIM ** -0.5 * LOG2E)).astype(BF16) for i in range(nq)]
    acc_ref[...] = jnp.zeros_like(acc_ref)
    carry_ref[...] = jnp.zeros_like(carry_ref)

    causal_gap = _iota2((n, tq), 1) - (_iota2((n, tq), 0) & (tq - 1))
    cs_mat = jnp.where(_iota2((tq, tq), 0) > _iota2((tq, tq), 1), 1.0, 0.0).astype(BF16)

    def body(state):
        it, _ = state
        js = [first_qb + i - it for i in range(nq)]
        jcs = [jnp.maximum(j, 0) for j in js]
        zs = [_dg(qs[i], k_ref[pl.ds(pl.multiple_of(jcs[i] * tq, tq), tq), :], NT) for i in range(nq)]
        log_keeps = [-(jnp.maximum(z, 0.0) + jnp.log2(1.0 + jnp.exp2(-jnp.abs(z)))) for z in zs]
        masks = [causal_gap < jnp.where(j >= 0, it * tq, -tq) for j in js]
        lkms = [jnp.where(m, lk, 0.0) for m, lk in zip(masks, log_keeps)]
        css = [_mm_exact_rhs(lkm, cs_mat, SB_CUMSUM_PIECES) for lkm in lkms]
        atts = []
        more = jnp.bool_(False)
        for i in range(nq):
            carry = carry_ref[i]
            att = jnp.where(masks[i], jnp.exp2(zs[i] + log_keeps[i] + css[i] + carry), 0.0).astype(BF16)
            atts.append(jnp.concatenate([att[h * tq:(h + 1) * tq] for h in range(N_HEADS)], axis=1))
            carry = carry + jnp.sum(lkms[i], axis=1, keepdims=True)
            carry_ref[i] = carry
            more = more | ((js[i] > 0) & (jnp.max(carry) > SB_LOG_UNDERFLOW * LOG2E))
        avs = [_dg(atts[i], vst_ref[jcs[i]]) for i in range(nq)]
        for i in range(nq):
            acc_ref[i] += avs[i]
        return it + 1, more.astype(jnp.int32)

    lax.while_loop(lambda s: s[1] > 0, body, (jnp.int32(0), jnp.int32(1)))
    for i in range(nq):
        o_ref[i * tq:(i + 1) * tq, :] = (acc_ref[i] * _silu(gate_ref[i * tq:(i + 1) * tq, :])).astype(o_ref.dtype)


def _sb(pqg, pkv):
    b, t, _ = pqg.shape
    tq = SB_BLOCK
    nq = SB_QBLOCKS
    blk = lambda cidx: pl.BlockSpec((None, nq * tq, GROUP), lambda i, j: (i, j, cidx))
    full = lambda cidx: pl.BlockSpec((None, t, GROUP), lambda i, j: (i, 0, cidx))
    return pl.pallas_call(
        _sb_kernel,
        grid=(b, t // (nq * tq)),
        in_specs=[blk(0), full(0), full(1), blk(1)],
        out_specs=pl.BlockSpec((None, nq * tq, GROUP), lambda i, j: (i, j, 0)),
        out_shape=jax.ShapeDtypeStruct((b, t, GROUP), BF16),
        scratch_shapes=[pltpu.VMEM((nq, tq, GROUP), F32), pltpu.VMEM((nq, N_HEADS * tq, tq), F32),
                        pltpu.VMEM((t // tq, N_HEADS * tq, GROUP), BF16)],
        compiler_params=pltpu.CompilerParams(dimension_semantics=("arbitrary", "arbitrary"),
                                             vmem_limit_bytes=VMEM_LIMIT),
        name="stick_breaking",
    )(pqg, pkv, pkv, pqg)


def _memkv_kernel(m_ref, g_ref, wk_ref, wv_ref, k_ref, v_ref):
    h = (_rms_scale(m_ref[...]) * g_ref[...]).astype(BF16)
    k_ref[...] = _dg(h, wk_ref[...]).astype(k_ref.dtype)
    v_ref[...] = _dg(h, wv_ref[...]).astype(v_ref.dtype)


def _memkv(mem2d, g, wk, wv, l):
    n, d = mem2d.shape
    tm = ROW_TILE
    row = lambda i: (i, 0)
    return pl.pallas_call(
        _memkv_kernel,
        grid=(n // tm,),
        in_specs=[pl.BlockSpec((tm, d), row), _layer(g, l), _layer(wk, l), _layer(wv, l)],
        out_specs=[pl.BlockSpec((tm, d), row)] * 2,
        out_shape=[jax.ShapeDtypeStruct((n, d), BF16)] * 2,
        compiler_params=pltpu.CompilerParams(dimension_semantics=("arbitrary",), vmem_limit_bytes=VMEM_LIMIT),
        name="mem_kv",
    )(mem2d, g, wk, wv)


def _post_kernel(x_ref, ya_ref, yb_ref, yc_ref, yd_ref, wout_ref, gx_ref, wq_ref, k_ref, v_ref, wo_ref,
                 gf_ref, o_ref, *, final):
    d = x_ref.shape[-1]
    hd = d // XATTN_HEADS
    x1 = x_ref[...]
    for g, y_ref in enumerate((ya_ref, yb_ref, yc_ref, yd_ref)):
        x1 = x1 + _dg(y_ref[...], wout_ref[g * GROUP:(g + 1) * GROUP, :])
    hx = (_rms_scale(x1) * gx_ref[...]).astype(BF16)
    q = _dg(hx, wq_ref[...])
    outs = []
    for h in range(XATTN_HEADS):
        sl = slice(h * hd, (h + 1) * hd)
        s = _dg(q[:, sl].astype(BF16), k_ref[:, sl], NT) * (hd ** -0.5)
        s = s - jnp.max(s, axis=-1, keepdims=True)
        e = jnp.exp(s)
        p = e / jnp.sum(e, axis=-1, keepdims=True)
        outs.append(_dg(p.astype(BF16), v_ref[:, sl]).astype(BF16))
    x2 = x1 + _dg(jnp.concatenate(outs, axis=-1), wo_ref[...])
    if final:
        x2 = _rms_scale(x2) * gf_ref[...]
    o_ref[...] = x2


def _post(x, ys, wout, gx, wq, kmem, vmem, wo, gf, l, final):
    b, t, d = x.shape
    m = kmem.shape[1]
    tm = POST_ROW_TILE
    rowblk = lambda w: pl.BlockSpec((None, tm, w), lambda i, j: (i, j, 0))
    return pl.pallas_call(
        functools.partial(_post_kernel, final=final),
        grid=(b, t // tm),
        in_specs=[rowblk(d)] + [rowblk(GROUP)] * 4
        + [_layer(wout, l), _layer(gx, l), _layer(wq, l),
           pl.BlockSpec((None, m, d), lambda i, j: (i, 0, 0)), pl.BlockSpec((None, m, d), lambda i, j: (i, 0, 0)),
           _layer(wo, l), pl.BlockSpec((1, d), lambda i, j: (0, 0))],
        out_specs=rowblk(d),
        out_shape=jax.ShapeDtypeStruct((b, t, d), F32),
        compiler_params=pltpu.CompilerParams(dimension_semantics=("arbitrary", "arbitrary"),
                                             vmem_limit_bytes=VMEM_LIMIT),
        name="outproj_xattn",
    )(x, *ys, wout, gx, wq, kmem, vmem, wo, gf)


def _rows8(*rows):
    rows = [r.reshape(r.shape[0], 1, -1).astype(F32) for r in rows]
    depth, _, width = rows[0].shape
    return jnp.concatenate(rows + [jnp.zeros((depth, 8 - len(rows), width), F32)], axis=1)


def kernel(x, mem, norm_mix, w_in, rwkv_mu, rwkv_w0, rwkv_w_up, rwkv_a0, rwkv_a_up, rwkv_k_k, rwkv_k_a,
           rwkv_r_k, rwkv_ln_g, rwkv_ln_b, conv_w, conv_b, conv_ln_g, conv_ln_b, hgrn_lb_logits, hgrn_norm_g,
           w_out, norm_xattn, norm_mem, xattn_wq, xattn_wk, xattn_wv, xattn_wo, norm_final):
    b, t, d = x.shape
    depth = w_in.shape[0]
    n_shift = 3 * GROUP + 2 * RWKV_LORA
    n_rwkv = n_shift + GROUP
    lb_soft = jax.nn.softmax(hgrn_lb_logits.astype(F32), axis=0)
    lower_bounds = jnp.cumsum(lb_soft, axis=0) - lb_soft[0]
    mem2d = mem.reshape(-1, d)
    pad_lora = RWKV_SHIFT_COLS - n_shift
    lora_rows = LANES - 2 * RWKV_LORA

    zeros = lambda *shape: jnp.zeros((depth,) + shape, F32)
    wa = jnp.concatenate([w_in[:, :, :n_shift], zeros(d, pad_lora), w_in[:, :, n_shift:n_rwkv]], axis=2).astype(BF16)
    wb = w_in[:, :, n_rwkv:n_rwkv + 3 * GROUP].astype(BF16)
    wc = w_in[:, :, n_rwkv + 3 * GROUP:n_rwkv + 7 * GROUP].astype(BF16)
    sb0 = n_rwkv + 7 * GROUP
    wqg = jnp.concatenate([w_in[:, :, sb0:sb0 + GROUP], w_in[:, :, sb0 + 3 * GROUP:]], axis=2).astype(BF16)
    wkv = w_in[:, :, sb0 + GROUP:sb0 + 3 * GROUP].astype(BF16)
    g_mix = norm_mix.reshape(depth, 1, d)
    mu = jnp.concatenate([rwkv_mu, zeros(pad_lora)], axis=1).reshape(depth, 1, -1)
    rvec = _rows8(rwkv_w0, rwkv_a0, rwkv_k_k, rwkv_k_a, rwkv_r_k, rwkv_ln_g, rwkv_ln_b)
    wup = jnp.concatenate([rwkv_w_up, zeros(RWKV_LORA + lora_rows, GROUP)], axis=1)
    aup = jnp.concatenate([zeros(RWKV_LORA, GROUP), rwkv_a_up, zeros(lora_rows, GROUP)], axis=1)
    cw = jnp.concatenate([conv_w, zeros(32 - CONV_WIDTH, GROUP)], axis=1)
    cvec = _rows8(conv_b, conv_ln_g, conv_ln_b)
    hvec = _rows8(lower_bounds, hgrn_norm_g)
    g_mem = norm_mem.reshape(depth, 1, d)
    g_x = norm_xattn.reshape(depth, 1, d)
    wk, wv, wq, wo, wout = (a.astype(BF16) for a in (xattn_wk, xattn_wv, xattn_wq, xattn_wo, w_out))

    for l in range(depth):
        pa, pqg, pkv, y_b, y_c = _mixin(x, g_mix, wa, wb, wc, wqg, wkv, cw, cvec, hvec, l)
        y_a = _rwkv(pa, mu, rvec, wup, aup, l)
        y_d = _sb(pqg, pkv)
        kmem, vmem = _memkv(mem2d, g_mem, wk, wv, l)
        x = _post(x, (y_a, y_b, y_c, y_d), wout, g_x, wq, kmem.reshape(b, -1, d), vmem.reshape(b, -1, d), wo,
                  norm_final.reshape(1, d), l, final=(l == depth - 1))
    return x
```

```python
import functools

import jax
import jax.numpy as jnp
from jax import lax
from jax.experimental import pallas as pl
from jax.experimental.pallas import tpu as pltpu

F32 = jnp.float32
BF16 = jnp.bfloat16

GROUP = 256
N_HEADS = 4
HEAD_DIM = GROUP // N_HEADS
RWKV_LORA = 32
CONV_WIDTH = 31
XATTN_HEADS = 4
RMS_EPS = 1e-6
LN_EPS = 1e-5
RWKV_LN_EPS = 64e-5

LANES = 128
RWKV_SHIFT_COLS = 3 * GROUP + LANES
RWKV_COLS = RWKV_SHIFT_COLS + GROUP

CHUNK = 64
RWKV_TILE = 8 * CHUNK
SUB = 16
HGRN_PASSES = 1
HGRN_DIAG_PIECES = 1
SB_BLOCK = 128
SB_QBLOCKS = 4
SB_CUMSUM_PIECES = 1
CONV_HALO = 32
ROW_TILE = 512
MIXIN_FILL_COLS = 512
POST_ROW_TILE = 1024
SB_LOG_UNDERFLOW = -120.0
LOG2E = 1.4426950408889634
NEG_BIG = -1e30
VMEM_LIMIT = 56 * 1024 * 1024

NN = ((1,), (0,))
NT = ((1,), (1,))
TN = ((0,), (0,))


def _pieces(x, n):
    out = []
    for _ in range(n - 1):
        p = x.astype(BF16)
        out.append(p)
        x = x - p.astype(F32)
    out.append(x.astype(BF16))
    return out


def _dg(a, b, dims=NN):
    return lax.dot_general(a, b, (dims, ((), ())), preferred_element_type=F32)


def _mm(a, b, dims=NN, passes=1):
    if passes == 1:
        return _dg(a.astype(BF16), b.astype(BF16), dims)
    a_hi, a_lo = _pieces(a, 2)
    b_hi, b_lo = _pieces(b, 2)
    return _dg(jnp.concatenate([a_hi, a_hi, a_lo], axis=dims[0][0]),
               jnp.concatenate([b_hi, b_lo, b_hi], axis=dims[1][0]), dims)


SUM_PIECES = 2


def _mm_exact_rhs(x, w_bf16, n=SUM_PIECES):
    return _dg(jnp.concatenate(_pieces(x, n), axis=1), jnp.concatenate([w_bf16] * n, axis=0))


def _mm_exact_lhs(w_bf16, x, n=SUM_PIECES):
    return _dg(jnp.concatenate([w_bf16] * n, axis=1), jnp.concatenate(_pieces(x, n), axis=0))


def _iota2(shape, dim):
    return lax.broadcasted_iota(jnp.int32, shape, dim)


def _log2(n):
    assert n & (n - 1) == 0
    return n.bit_length() - 1


HEAD_SHIFT = _log2(HEAD_DIM)


def _same_head_ones(n=GROUP):
    r = _iota2((n, n), 0) >> HEAD_SHIFT
    c = _iota2((n, n), 1) >> HEAD_SHIFT
    return jnp.where(r == c, 1.0, 0.0).astype(BF16)


def _lower_ones(n):
    return jnp.where(_iota2((n, n), 1) <= _iota2((n, n), 0), 1.0, 0.0).astype(BF16)


def _stack_heads(x):
    head = _iota2(x.shape, 1) >> HEAD_SHIFT
    return jnp.concatenate([jnp.where(head == h, x, 0.0) for h in range(N_HEADS)], axis=0)


PAIR_HEADS = LANES // HEAD_DIM


def _stack_pair(x, g):
    xg = x[:, g * LANES:(g + 1) * LANES]
    head = _iota2(xg.shape, 1) >> HEAD_SHIFT
    return jnp.concatenate([jnp.where(head == h, xg, 0.0) for h in range(PAIR_HEADS)], axis=0)


def _unstack_heads(y, t):
    return functools.reduce(lambda s, u: s + u, [y[h * t:(h + 1) * t] for h in range(N_HEADS)])


def _own_head_lanes(t):
    return (_iota2((N_HEADS * t, GROUP), 1) >> HEAD_SHIFT) == (_iota2((N_HEADS * t, GROUP), 0) >> _log2(t))


def _sigmoid(x):
    return 1.0 / (1.0 + jnp.exp(-x))


def _silu(x):
    return x * _sigmoid(x)


def _softplus(x):
    return jnp.maximum(x, 0.0) + jnp.log(1.0 + jnp.exp(-jnp.abs(x)))


def _rms_scale(x):
    return x * lax.rsqrt(jnp.mean(x * x, axis=-1, keepdims=True) + RMS_EPS)


def _layer(arr, l):
    shape = arr.shape[1:]
    return pl.BlockSpec((None,) + shape, lambda *_: (l,) + (0,) * len(shape))


def _rwkv_rows(p, prev_row, mu, vec, wup, aup):
    row = _iota2(p.shape, 0)
    p_prev = jnp.where(row == 0, prev_row, pltpu.roll(p, 1, 0))
    m = p + (p_prev - p) * mu
    r = m[:, 0:GROUP]
    k = m[:, GROUP:2 * GROUP]
    v = m[:, 2 * GROUP:3 * GROUP]
    lora = m[:, 3 * GROUP:RWKV_SHIFT_COLS]
    w0, a0, k_k, k_a, r_k = vec[0:1, :], vec[1:2, :], vec[2:3, :], vec[3:4, :], vec[4:5, :]
    gseg = _same_head_ones()
    w_log = -_softplus(-(w0 + _mm(jnp.tanh(lora), wup))) - 0.5
    lw = -jnp.exp(w_log)
    a_sig = _sigmoid(a0 + _mm(lora, aup))
    kk = k * k_k
    kk = kk / jnp.maximum(jnp.sqrt(_mm_exact_rhs(kk * kk, gseg)), 1e-12)
    k2 = k * (1.0 + (a_sig - 1.0) * k_a)
    bonus = _mm_exact_rhs(r * k2 * r_k, gseg) * v
    return r, k2, v, kk, a_sig, lw, bonus


def _rwkv_prep(chains):
    c = CHUNK
    n = PAIR_HEADS * c
    ltri = _lower_ones(c)
    ri = _iota2((n, n), 0)
    ci = _iota2((n, n), 1)
    cshift = _log2(c)
    same = (ri >> cshift) == (ci >> cshift)
    strict = same & (ci < ri)
    incl = same & (ci <= ri)
    fs = []
    for r, k2, v, kk, a_sig, lw in chains:
        cum = _mm_exact_lhs(ltri, lw)
        cum_last = cum[c - 1:c, :]
        g_inv = jnp.exp(-cum)
        g_rem = jnp.exp(cum_last - cum)
        a_dec = -kk * jnp.exp(cum - lw)
        r_dec = r * jnp.exp(cum)
        b = kk * a_sig
        decay = jnp.exp(cum_last)
        for g in range(N_HEADS // PAIR_HEADS):
            fs.append(dict(a_t=_stack_pair(a_dec, g), r_t=_stack_pair(r_dec, g),
                           b_t=_stack_pair(b * g_inv, g), k_t=_stack_pair(k2 * g_inv, g),
                           b_h=_stack_pair(b * g_rem, g), k_h=_stack_pair(k2 * g_rem, g),
                           v_s=_stack_pair(v, g), decay=decay[:, g * LANES:(g + 1) * LANES]))
    pairs = [_dg(jnp.concatenate([f["a_t"], f["r_t"]], axis=0).astype(BF16),
                 jnp.concatenate([f["b_t"], f["k_t"]], axis=0).astype(BF16), NT) for f in fs]
    for f, m in zip(fs, pairs):
        f["a_ab"] = jnp.where(strict, m[0:n, 0:n], 0.0)
        f["a_ak"] = jnp.where(strict, m[0:n, n:2 * n], 0.0)
        f["a_rb"] = jnp.where(incl, m[n:2 * n, 0:n], 0.0)
        f["a_rk"] = jnp.where(incl, m[n:2 * n, n:2 * n], 0.0)

    eye = jnp.where(ri == ci, 1.0, 0.0)
    for f in fs:
        f["tinv"] = eye + jnp.where((ri >> 1) == (ci >> 1), f["a_ab"], 0.0)
    for s in range(2, cshift + 1):
        lower_left = ((ri >> s) == (ci >> s)) & ((ri >> (s - 1)) != (ci >> (s - 1)))
        xs = [_mm(jnp.where(lower_left, f["a_ab"], 0.0), f["tinv"]) for f in fs]
        xs = [_mm(f["tinv"], x) for f, x in zip(fs, xs)]
        for f, x in zip(fs, xs):
            f["tinv"] = f["tinv"] + x
    for f in fs:
        f["w_m"] = _mm(f["tinv"], f["a_t"])
    xs = [_mm(f["a_ak"], f["v_s"]) for f in fs]
    for f, x in zip(fs, xs):
        f["z_m"] = _mm(f["tinv"], x)
    for f in fs:
        f["a_r"] = jnp.concatenate([f["a_rb"], f["a_rk"]], axis=1).astype(BF16)
        f["bk_h"] = jnp.concatenate([f["b_h"], f["k_h"]], axis=0).astype(BF16)
        f["wr"] = jnp.concatenate([f["w_m"], f["r_t"]], axis=0).astype(BF16)
        f["v_s"] = f["v_s"].astype(BF16)
    return fs


def _rwkv_step(fs, states):
    n = PAIR_HEADS * CHUNK
    sb = [s.astype(BF16) for s in states]
    wrs = [_dg(f["wr"], s, NT) for f, s in zip(fs, sb)]
    uvs = [jnp.concatenate([(wr[0:n] + f["z_m"]).astype(BF16), f["v_s"]], axis=0) for f, wr in zip(fs, wrs)]
    ys = [wr[n:2 * n] + _dg(f["a_r"], uv) for f, wr, uv in zip(fs, wrs, uvs)]
    new = [s * f["decay"] + _dg(uv, f["bk_h"], TN) for f, uv, s in zip(fs, uvs, states)]
    return ys, new


def _rwkv_kernel(p_ref, mu_ref, vec_ref, wup_ref, aup_ref, o_ref, s_ref, prev_ref):
    @pl.when(pl.program_id(0) == 0)
    def _():
        s_ref[...] = jnp.zeros_like(s_ref)
        prev_ref[...] = jnp.zeros_like(prev_ref)

    c = CHUNK
    nb, tile = p_ref.shape[0], p_ref.shape[1]
    vec = vec_ref[...]
    ln_g, ln_b = vec[5:6, :], vec[6:7, :]
    gseg = _same_head_ones()
    nch = tile // c
    rows = []
    for bi in range(nb):
        p = p_ref[bi, :, 0:RWKV_SHIFT_COLS]
        rows.append(_rwkv_rows(p, prev_ref[bi, 0:1, :], mu_ref[...], vec, wup_ref[...], aup_ref[...]))
        prev_ref[bi, 0:1, :] = p[tile - 1:tile, :]
    npair = N_HEADS // PAIR_HEADS
    factors = _rwkv_prep([tuple(a[j * c:(j + 1) * c] for a in rows[bi][:6]) for j in range(nch) for bi in range(nb)])
    states = [s_ref[bi, g] for bi in range(nb) for g in range(npair)]
    y_parts = [[] for _ in range(nb)]
    for j in range(nch):
        ys, states = _rwkv_step(factors[j * nb * npair:(j + 1) * nb * npair], states)
        for bi in range(nb):
            pair_ys = ys[bi * npair:(bi + 1) * npair]
            y_parts[bi].append(jnp.concatenate(
                [functools.reduce(lambda s, t: s + t, [y[h * c:(h + 1) * c] for h in range(PAIR_HEADS)])
                 for y in pair_ys], axis=1))
    for bi in range(nb):
        for g in range(npair):
            s_ref[bi, g] = states[bi * npair + g]
        y = jnp.concatenate(y_parts[bi], axis=0)
        mean = _mm_exact_rhs(y, gseg) * (1.0 / HEAD_DIM)
        yc = y - mean
        var = _mm_exact_rhs(yc * yc, gseg) * (1.0 / HEAD_DIM)
        yn = yc * lax.rsqrt(var + RWKV_LN_EPS) * ln_g + ln_b
        gate = p_ref[bi, :, RWKV_SHIFT_COLS:RWKV_COLS]
        o_ref[bi] = ((yn + rows[bi][6]) * _silu(gate)).astype(o_ref.dtype)


def _rwkv(pa, mu, vec, wup, aup, l):
    b, t, _ = pa.shape
    return pl.pallas_call(
        _rwkv_kernel,
        grid=(t // RWKV_TILE,),
        in_specs=[pl.BlockSpec((b, RWKV_TILE, RWKV_COLS), lambda j: (0, j, 0))]
        + [_layer(a, l) for a in (mu, vec, wup, aup)],
        out_specs=pl.BlockSpec((b, RWKV_TILE, GROUP), lambda j: (0, j, 0)),
        out_shape=jax.ShapeDtypeStruct((b, t, GROUP), BF16),
        scratch_shapes=[pltpu.VMEM((b, N_HEADS // PAIR_HEADS, LANES, LANES), F32),
                        pltpu.VMEM((b, 8, RWKV_SHIFT_COLS), F32)],
        compiler_params=pltpu.CompilerParams(dimension_semantics=("arbitrary",), vmem_limit_bytes=VMEM_LIMIT),
        name="rwkv7",
    )(pa, mu, vec, wup, aup)


def _conv_body(p_ref, w_ref, vec_ref, o_ref, u_ref, sh_ref):
    tt = p_ref.shape[0]
    sub = 8
    u_ref[CONV_HALO:CONV_HALO + tt, :] = p_ref[:, 0:GROUP] * _sigmoid(p_ref[:, GROUP:2 * GROUP])
    base = CONV_HALO - (CONV_WIDTH - 1)
    rows = sh_ref.shape[1]
    for b in range(1, sub):
        sh_ref[b - 1] = u_ref[b:b + rows, :]
    acc = jnp.zeros((tt, GROUP), F32)
    for j in range(CONV_WIDTH):
        a, b = divmod(base + j, sub)
        src = u_ref[a * sub:a * sub + tt, :] if b == 0 else sh_ref[b - 1, a * sub:a * sub + tt, :]
        acc = acc + src * w_ref[j:j + 1, :]
    u_ref[0:CONV_HALO, :] = u_ref[tt:tt + CONV_HALO, :]
    y = acc + vec_ref[0:1, :]
    mu = jnp.mean(y, axis=-1, keepdims=True)
    yc = y - mu
    var = jnp.mean(yc * yc, axis=-1, keepdims=True)
    yn = yc * lax.rsqrt(var + LN_EPS) * vec_ref[1:2, :] + vec_ref[2:3, :]
    o_ref[...] = (_silu(yn) * _silu(p_ref[:, 2 * GROUP:3 * GROUP])).astype(o_ref.dtype)


def _hgrn_diag_rows(i, s):
    half = SUB // 2
    per_sub = SUB * half + half * half
    if s < half:
        return i * per_sub + s * SUB, SUB, 0
    return i * per_sub + half * SUB + (s - half) * half, half, half


def _hgrn_body(p_ref, vec_ref, o_ref, s_ref, b_ref, k_ref, v_ref, p_buf, r_buf, fillers=()):
    fillers = list(fillers)

    def fill():
        if fillers:
            fillers.pop(0)()

    c = CHUNK
    nb, tile = p_ref.shape[0], p_ref.shape[1]
    nch = tile // c
    nsub = c // SUB
    sshift = _log2(SUB)
    lb, norm_g = vec_ref[0:1, :], vec_ref[1:2, :]
    gseg = _same_head_ones()
    ltri = _lower_ones(c)
    chains = [(bi, j) for j in range(nch) for bi in range(nb)]
    q, kf, v, bcum = [], [], [], []
    for bi, j in chains:
        rows = slice(j * c, (j + 1) * c)
        q.append(_silu(p_ref[bi, rows, 0:GROUP]))
        f = lb + (1.0 - lb) * _sigmoid(p_ref[bi, rows, GROUP:2 * GROUP])
        kf.append(1.0 - f)
        v.append(p_ref[bi, rows, 2 * GROUP:3 * GROUP])
        bcum.append(jnp.log(f))
    bcum = [_mm_exact_lhs(ltri, lf) for lf in bcum]
    for n in range(len(chains)):
        b_ref[n] = bcum[n]
        k_ref[n] = kf[n]
        v_ref[n] = v[n]
    b_last = [b[c - 1:c, :] for b in bcum]
    same_head = (_iota2((GROUP, GROUP), 0) >> HEAD_SHIFT) == (_iota2((GROUP, GROUP), 1) >> HEAD_SHIFT)
    kv = [jnp.where(same_head, _mm(v[n], kf[n] * jnp.exp(b_last[n] - bcum[n]), TN, HGRN_PASSES), 0.0)
          for n in range(len(chains))]

    trow = _iota2((c, GROUP), 0)
    tsub = trow >> sshift
    q_off, k_cat = [], []
    for n in range(len(chains)):
        beta = b_ref[n, SUB - 1:SUB, :]
        for i in range(2, nsub):
            beta = jnp.where(tsub == i, b_ref[n, i * SUB - 1:i * SUB, :], beta)
        q_off.append(_stack_heads(q[n] * jnp.exp(jnp.where(tsub >= 1, bcum[n] - beta, NEG_BIG))))
        k_cat.append(jnp.concatenate(
            [kf[n] * jnp.exp(jnp.where(trow < i * SUB, b_ref[n, i * SUB - 1:i * SUB, :] - bcum[n], NEG_BIG))
             for i in range(1, nsub)], axis=0))
    att = [_mm(qo, kc, NT, HGRN_PASSES) for qo, kc in zip(q_off, k_cat)]
    arow = (_iota2(att[0].shape, 0) & (c - 1)) >> sshift
    acol = _iota2(att[0].shape, 1) >> _log2(c)
    att = [jnp.where(acol + 1 == arow, a, 0.0) for a in att]
    o_off = [_mm(a, jnp.concatenate([vn] * (nsub - 1), axis=0)) for a, vn in zip(att, v)]
    own = _own_head_lanes(c)
    o_off = [_unstack_heads(jnp.where(own, o, 0.0), c) for o in o_off]

    half = SUB // 2
    for n in range(len(chains)):
        for i in range(nsub):
            for s in range(SUB):
                src = i * SUB + s
                start, nrows, first = _hgrn_diag_rows(i, s)
                tgt = slice(i * SUB + first, (i + 1) * SUB)
                tau = _iota2((nrows, GROUP), 0) + first
                prod = (q[n][tgt] * jnp.exp(jnp.where(tau >= s, bcum[n][tgt] - b_ref[n, src:src + 1, :], NEG_BIG))
                        * k_ref[n, src:src + 1, :])
                p_buf[n, start:start + nrows, :] = prod
        fill()
    for n in range(len(chains)):
        r_buf[n] = _mm_exact_rhs(p_buf[n], gseg, HGRN_DIAG_PIECES)
    o_diag = []
    for n in range(len(chains)):
        fill()
        parts = []
        for i in range(nsub):
            lo = jnp.zeros((half, GROUP), F32)
            hi = jnp.zeros((half, GROUP), F32)
            for s in range(SUB):
                src = i * SUB + s
                start, nrows, first = _hgrn_diag_rows(i, s)
                vs = v_ref[n, src:src + 1, :]
                if first == 0:
                    lo = lo + r_buf[n, start:start + half, :] * vs
                    hi = hi + r_buf[n, start + half:start + SUB, :] * vs
                else:
                    hi = hi + r_buf[n, start:start + half, :] * vs
            parts += [lo, hi]
        o_diag.append(jnp.concatenate(parts, axis=0))

    states = [s_ref[bi] for bi in range(nb)]
    for j in range(nch):
        ns = range(j * nb, (j + 1) * nb)
        o_inter = [_mm(q[n] * jnp.exp(bcum[n]), states[n - j * nb], NT, HGRN_PASSES) for n in ns]
        states = [states[n - j * nb] * jnp.exp(b_last[n]) + kv[n] for n in ns]
        for n, oi in zip(ns, o_inter):
            bi = n - j * nb
            o = oi + o_off[n] + o_diag[n]
            ms = _mm_exact_rhs(o * o, gseg) * (1.0 / HEAD_DIM)
            o = o * lax.rsqrt(ms + RMS_EPS) * norm_g
            gate = p_ref[bi, j * c:(j + 1) * c, 3 * GROUP:4 * GROUP]
            o_ref[bi, j * c:(j + 1) * c, :] = (o * _silu(gate)).astype(o_ref.dtype)
    for bi in range(nb):
        s_ref[bi] = states[bi]
    while fillers:
        fill()


def _mixin_kernel(x_ref, g_ref, wa_ref, wb_ref, wc_ref, wqg_ref, wkv_ref, cw_ref, cvec_ref, hvec_ref,
                  pa_ref, pqg_ref, pkv_ref, yb_ref, yc_ref,
                  pb_s, pc_s, u_ref, sh_ref, s_ref, b_ref, k_ref, v_ref, p_buf, r_buf):
    @pl.when(pl.program_id(1) == 0)
    def _():
        u_ref[0:CONV_HALO, :] = jnp.zeros((CONV_HALO, GROUP), F32)
        s_ref[...] = jnp.zeros_like(s_ref)

    hb = (_rms_scale(x_ref[...]) * g_ref[...]).astype(BF16)
    pb_s[...] = _dg(hb, wb_ref[...])
    pc_s[0] = _dg(hb, wc_ref[...])

    def project(w_ref, o_ref, c0, c1):
        def run():
            o_ref[:, c0:c1] = _dg(hb, w_ref[:, c0:c1]).astype(o_ref.dtype)
        return run

    pieces = [project(w_ref, o_ref, c0, min(c0 + MIXIN_FILL_COLS, w_ref.shape[1]))
              for w_ref, o_ref in ((wa_ref, pa_ref), (wqg_ref, pqg_ref), (wkv_ref, pkv_ref))
              for c0 in range(0, w_ref.shape[1], MIXIN_FILL_COLS)]
    pieces[0]()
    _conv_body(pb_s, cw_ref, cvec_ref, yb_ref, u_ref, sh_ref)
    _hgrn_body(pc_s, hvec_ref, yc_ref, s_ref, b_ref, k_ref, v_ref, p_buf, r_buf, fillers=pieces[1:])


def _mixin(x, g, wa, wb, wc, wqg, wkv, cw, cvec, hvec, l):
    b, t, d = x.shape
    tm = ROW_TILE
    nchains = tm // CHUNK
    diag_rows = (CHUNK // SUB) * (SUB * SUB // 2 + SUB * SUB // 4)
    rowblk = lambda w: pl.BlockSpec((None, tm, w), lambda i, j: (i, j, 0))
    widths = (wa.shape[-1], wqg.shape[-1], wkv.shape[-1])
    return pl.pallas_call(
        _mixin_kernel,
        grid=(b, t // tm),
        in_specs=[rowblk(d)] + [_layer(a, l) for a in (g, wa, wb, wc, wqg, wkv, cw, cvec, hvec)],
        out_specs=[rowblk(w) for w in widths] + [rowblk(GROUP), pl.BlockSpec((1, tm, GROUP), lambda i, j: (i, j, 0))],
        out_shape=[jax.ShapeDtypeStruct((b, t, w), dt) for w, dt in zip(widths, (F32, F32, BF16))]
        + [jax.ShapeDtypeStruct((b, t, GROUP), BF16)] * 2,
        scratch_shapes=[pltpu.VMEM((tm, wb.shape[-1]), F32), pltpu.VMEM((1, tm, wc.shape[-1]), F32),
                        pltpu.VMEM((CONV_HALO + tm, GROUP), F32), pltpu.VMEM((7, CONV_HALO + tm - 8, GROUP), F32),
                        pltpu.VMEM((1, GROUP, GROUP), F32)] + [pltpu.VMEM((nchains, CHUNK, GROUP), F32)] * 3
        + [pltpu.VMEM((nchains, diag_rows, GROUP), F32)] * 2,
        compiler_params=pltpu.CompilerParams(dimension_semantics=("arbitrary", "arbitrary"),
                                             vmem_limit_bytes=VMEM_LIMIT),
        name="inproj_conv_hgrn",
    )(x, g, wa, wb, wc, wqg, wkv, cw, cvec, hvec)


def _sb_kernel(q_ref, k_ref, v_ref, gate_ref, o_ref, acc_ref, carry_ref, vst_ref):
    tq = SB_BLOCK
    nq = SB_QBLOCKS
    n = N_HEADS * tq
    first_qb = pl.program_id(1) * nq

    @pl.when(pl.program_id(1) == 0)
    def _():
        def fill(j, c):
            vst_ref[j] = _stack_heads(v_ref[pl.ds(pl.multiple_of(j * tq, tq), tq), :])
            return c
        lax.fori_loop(0, v_ref.shape[0] // tq, fill, 0)

    qs = [_stack_heads(q_ref[i * tq:(i + 1) * tq, :] * (HEAD_DIM ** -0.5 * LOG2E)).astype(BF16) for i in range(nq)]
    acc_ref[...] = jnp.zeros_like(acc_ref)
    carry_ref[...] = jnp.zeros_like(carry_ref)

    causal_gap = _iota2((n, tq), 1) - (_iota2((n, tq), 0) & (tq - 1))
    cs_mat = jnp.where(_iota2((tq, tq), 0) > _iota2((tq, tq), 1), 1.0, 0.0).astype(BF16)

    def body(state):
        it, _ = state
        js = [first_qb + i - it for i in range(nq)]
        jcs = [jnp.maximum(j, 0) for j in js]
        zs = [_dg(qs[i], k_ref[pl.ds(pl.multiple_of(jcs[i] * tq, tq), tq), :], NT) for i in range(nq)]
        log_keeps = [-(jnp.maximum(z, 0.0) + jnp.log2(1.0 + jnp.exp2(-jnp.abs(z)))) for z in zs]
        masks = [causal_gap < jnp.where(j >= 0, it * tq, -tq) for j in js]
        lkms = [jnp.where(m, lk, 0.0) for m, lk in zip(masks, log_keeps)]
        css = [_mm_exact_rhs(lkm, cs_mat, SB_CUMSUM_PIECES) for lkm in lkms]
        atts = []
        more = jnp.bool_(False)
        for i in range(nq):
            carry = carry_ref[i]
            att = jnp.where(masks[i], jnp.exp2(zs[i] + log_keeps[i] + css[i] + carry), 0.0).astype(BF16)
            atts.append(jnp.concatenate([att[h * tq:(h + 1) * tq] for h in range(N_HEADS)], axis=1))
            carry = carry + jnp.sum(lkms[i], axis=1, keepdims=True)
            carry_ref[i] = carry
            more = more | ((js[i] > 0) & (jnp.max(carry) > SB_LOG_UNDERFLOW * LOG2E))
        avs = [_dg(atts[i], vst_ref[jcs[i]]) for i in range(nq)]
        for i in range(nq):
            acc_ref[i] += avs[i]
        return it + 1, more.astype(jnp.int32)

    lax.while_loop(lambda s: s[1] > 0, body, (jnp.int32(0), jnp.int32(1)))
    for i in range(nq):
        o_ref[i * tq:(i + 1) * tq, :] = (acc_ref[i] * _silu(gate_ref[i * tq:(i + 1) * tq, :])).astype(o_ref.dtype)


def _sb(pqg, pkv):
    b, t, _ = pqg.shape
    tq = SB_BLOCK
    nq = SB_QBLOCKS
    blk = lambda cidx: pl.BlockSpec((None, nq * tq, GROUP), lambda i, j: (i, j, cidx))
    full = lambda cidx: pl.BlockSpec((None, t, GROUP), lambda i, j: (i, 0, cidx))
    return pl.pallas_call(
        _sb_kernel,
        grid=(b, t // (nq * tq)),
        in_specs=[blk(0), full(0), full(1), blk(1)],
        out_specs=pl.BlockSpec((None, nq * tq, GROUP), lambda i, j: (i, j, 0)),
        out_shape=jax.ShapeDtypeStruct((b, t, GROUP), BF16),
        scratch_shapes=[pltpu.VMEM((nq, tq, GROUP), F32), pltpu.VMEM((nq, N_HEADS * tq, tq), F32),
                        pltpu.VMEM((t // tq, N_HEADS * tq, GROUP), BF16)],
        compiler_params=pltpu.CompilerParams(dimension_semantics=("arbitrary", "arbitrary"),
                                             vmem_limit_bytes=VMEM_LIMIT),
        name="stick_breaking",
    )(pqg, pkv, pkv, pqg)


def _memkv_kernel(m_ref, g_ref, wk_ref, wv_ref, k_ref, v_ref):
    h = (_rms_scale(m_ref[...]) * g_ref[...]).astype(BF16)
    k_ref[...] = _dg(h, wk_ref[...]).astype(k_ref.dtype)
    v_ref[...] = _dg(h, wv_ref[...]).astype(v_ref.dtype)


def _memkv(mem2d, g, wk, wv, l):
    n, d = mem2d.shape
    tm = ROW_TILE
    row = lambda i: (i, 0)
    return pl.pallas_call(
        _memkv_kernel,
        grid=(n // tm,),
        in_specs=[pl.BlockSpec((tm, d), row), _layer(g, l), _layer(wk, l), _layer(wv, l)],
        out_specs=[pl.BlockSpec((tm, d), row)] * 2,
        out_shape=[jax.ShapeDtypeStruct((n, d), BF16)] * 2,
        compiler_params=pltpu.CompilerParams(dimension_semantics=("arbitrary",), vmem_limit_bytes=VMEM_LIMIT),
        name="mem_kv",
    )(mem2d, g, wk, wv)


def _post_kernel(x_ref, ya_ref, yb_ref, yc_ref, yd_ref, wout_ref, gx_ref, wq_ref, k_ref, v_ref, wo_ref,
                 gf_ref, o_ref, *, final):
    d = x_ref.shape[-1]
    hd = d // XATTN_HEADS
    x1 = x_ref[...]
    for g, y_ref in enumerate((ya_ref, yb_ref, yc_ref, yd_ref)):
        x1 = x1 + _dg(y_ref[...], wout_ref[g * GROUP:(g + 1) * GROUP, :])
    hx = (_rms_scale(x1) * gx_ref[...]).astype(BF16)
    q = _dg(hx, wq_ref[...])
    outs = []
    for h in range(XATTN_HEADS):
        sl = slice(h * hd, (h + 1) * hd)
        s = _dg(q[:, sl].astype(BF16), k_ref[:, sl], NT) * (hd ** -0.5)
        s = s - jnp.max(s, axis=-1, keepdims=True)
        e = jnp.exp(s)
        p = e / jnp.sum(e, axis=-1, keepdims=True)
        outs.append(_dg(p.astype(BF16), v_ref[:, sl]).astype(BF16))
    x2 = x1 + _dg(jnp.concatenate(outs, axis=-1), wo_ref[...])
    if final:
        x2 = _rms_scale(x2) * gf_ref[...]
    o_ref[...] = x2


def _post(x, ys, wout, gx, wq, kmem, vmem, wo, gf, l, final):
    b, t, d = x.shape
    m = kmem.shape[1]
    tm = POST_ROW_TILE
    rowblk = lambda w: pl.BlockSpec((None, tm, w), lambda i, j: (i, j, 0))
    return pl.pallas_call(
        functools.partial(_post_kernel, final=final),
        grid=(b, t // tm),
        in_specs=[rowblk(d)] + [rowblk(GROUP)] * 4
        + [_layer(wout, l), _layer(gx, l), _layer(wq, l),
           pl.BlockSpec((None, m, d), lambda i, j: (i, 0, 0)), pl.BlockSpec((None, m, d), lambda i, j: (i, 0, 0)),
           _layer(wo, l), pl.BlockSpec((1, d), lambda i, j: (0, 0))],
        out_specs=rowblk(d),
        out_shape=jax.ShapeDtypeStruct((b, t, d), F32),
        compiler_params=pltpu.CompilerParams(dimension_semantics=("arbitrary", "arbitrary"),
                                             vmem_limit_bytes=VMEM_LIMIT),
        name="outproj_xattn",
    )(x, *ys, wout, gx, wq, kmem, vmem, wo, gf)


def _rows8(*rows):
    rows = [r.reshape(r.shape[0], 1, -1).astype(F32) for r in rows]
    depth, _, width = rows[0].shape
    return jnp.concatenate(rows + [jnp.zeros((depth, 8 - len(rows), width), F32)], axis=1)


def kernel(x, mem, norm_mix, w_in, rwkv_mu, rwkv_w0, rwkv_w_up, rwkv_a0, rwkv_a_up, rwkv_k_k, rwkv_k_a,
           rwkv_r_k, rwkv_ln_g, rwkv_ln_b, conv_w, conv_b, conv_ln_g, conv_ln_b, hgrn_lb_logits, hgrn_norm_g,
           w_out, norm_xattn, norm_mem, xattn_wq, xattn_wk, xattn_wv, xattn_wo, norm_final):
    b, t, d = x.shape
    depth = w_in.shape[0]
    n_shift = 3 * GROUP + 2 * RWKV_LORA
    n_rwkv = n_shift + GROUP
    lb_soft = jax.nn.softmax(hgrn_lb_logits.astype(F32), axis=0)
    lower_bounds = jnp.cumsum(lb_soft, axis=0) - lb_soft[0]
    mem2d = mem.reshape(-1, d)
    pad_lora = RWKV_SHIFT_COLS - n_shift
    lora_rows = LANES - 2 * RWKV_LORA

    zeros = lambda *shape: jnp.zeros((depth,) + shape, F32)
    wa = jnp.concatenate([w_in[:, :, :n_shift], zeros(d, pad_lora), w_in[:, :, n_shift:n_rwkv]], axis=2).astype(BF16)
    wb = w_in[:, :, n_rwkv:n_rwkv + 3 * GROUP].astype(BF16)
    wc = w_in[:, :, n_rwkv + 3 * GROUP:n_rwkv + 7 * GROUP].astype(BF16)
    sb0 = n_rwkv + 7 * GROUP
    wqg = jnp.concatenate([w_in[:, :, sb0:sb0 + GROUP], w_in[:, :, sb0 + 3 * GROUP:]], axis=2).astype(BF16)
    wkv = w_in[:, :, sb0 + GROUP:sb0 + 3 * GROUP].astype(BF16)
    g_mix = norm_mix.reshape(depth, 1, d)
    mu = jnp.concatenate([rwkv_mu, zeros(pad_lora)], axis=1).reshape(depth, 1, -1)
    rvec = _rows8(rwkv_w0, rwkv_a0, rwkv_k_k, rwkv_k_a, rwkv_r_k, rwkv_ln_g, rwkv_ln_b)
    wup = jnp.concatenate([rwkv_w_up, zeros(RWKV_LORA + lora_rows, GROUP)], axis=1)
    aup = jnp.concatenate([zeros(RWKV_LORA, GROUP), rwkv_a_up, zeros(lora_rows, GROUP)], axis=1)
    cw = jnp.concatenate([conv_w, zeros(32 - CONV_WIDTH, GROUP)], axis=1)
    cvec = _rows8(conv_b, conv_ln_g, conv_ln_b)
    hvec = _rows8(lower_bounds, hgrn_norm_g)
    g_mem = norm_mem.reshape(depth, 1, d)
    g_x = norm_xattn.reshape(depth, 1, d)
    wk, wv, wq, wo, wout = (a.astype(BF16) for a in (xattn_wk, xattn_wv, xattn_wq, xattn_wo, w_out))

    for l in range(depth):
        pa, pqg, pkv, y_b, y_c = _mixin(x, g_mix, wa, wb, wc, wqg, wkv, cw, cvec, hvec, l)
        y_a = _rwkv(pa, mu, rvec, wup, aup, l)
        y_d = _sb(pqg, pkv)
        kmem, vmem = _memkv(mem2d, g_mem, wk, wv, l)
        x = _post(x, (y_a, y_b, y_c, y_d), wout, g_x, wq, kmem.reshape(b, -1, d), vmem.reshape(b, -1, d), wo,
                  norm_final.reshape(1, d), l, final=(l == depth - 1))
    return x
```
